```python
import jax, jax.numpy as jnp
from jax import lax
import numpy as np

D_MODEL = 1024
BATCH = 8
SEQ = 8192
DEPTH = 2
DEC_BATCH = 8
DEC_SEQ = 16
PAST_LEN = 1024

CHUNK = 64
N_MIXERS = 2
N_SSD = (DEPTH + 1) // 2
N_CONV = DEPTH // 2
D_FF = 2816
SSD_EXPAND = 2
D_INNER = SSD_EXPAND * D_MODEL
SSD_HEAD_DIM = 64
SSD_HEADS = D_INNER // SSD_HEAD_DIM
SSD_GROUPS = 4
SSD_STATE = 128
SSD_CONV_W = 4
SSD_CONV_DIM = D_INNER + 2 * SSD_GROUPS * SSD_STATE
SSD_PROJ = 2 * D_INNER + 2 * SSD_GROUPS * SSD_STATE + SSD_HEADS
D_CONV = D_MODEL
CONV_W = 31
N_MOD = 9
EPS = 1e-6

kernel_name = 'hybrid_ssd_conformer_stream_step'


def rms_norm(x, g):
    xf = x.astype(jnp.float32)
    y = xf * lax.rsqrt(jnp.mean(xf * xf, axis=-1, keepdims=True) + EPS)
    return (y * g.astype(jnp.float32)).astype(x.dtype)


def layer_norm(x, g, b):
    xf = x.astype(jnp.float32)
    mu = jnp.mean(xf, axis=-1, keepdims=True)
    xc = xf - mu
    y = xc * lax.rsqrt(jnp.mean(xc * xc, axis=-1, keepdims=True) + EPS)
    return (y * g.astype(jnp.float32) + b.astype(jnp.float32)).astype(x.dtype)


def swiglu(h, w1, w3, w2):
    return (jax.nn.silu(h @ w1) * (h @ w3)) @ w2


def causal_dwconv(x, buf, w, bias):
    k = w.shape[0]
    xp = jnp.concatenate([buf.astype(x.dtype), x], axis=1)
    y = lax.conv_general_dilated(
        xp, w[:, None, :].astype(x.dtype), window_strides=(1,), padding='VALID',
        dimension_numbers=('NWC', 'WIO', 'NWC'), feature_group_count=x.shape[-1])
    new_buf = xp[:, xp.shape[1] - (k - 1):]
    return y + bias.astype(x.dtype), new_buf


def ssd_chunked(x, dt, a, bm, cm, h0, chunk):
    b, seq, nh, p = x.shape
    g, n = bm.shape[-2:]
    hg = nh // g
    nc = seq // chunk
    xc = jnp.moveaxis(x.astype(jnp.float32).reshape(b, nc, chunk, g, hg, p), 1, 0)
    dtc = jnp.moveaxis(dt.reshape(b, nc, chunk, g, hg), 1, 0)
    bc = jnp.moveaxis(bm.astype(jnp.float32).reshape(b, nc, chunk, g, n), 1, 0)
    cc = jnp.moveaxis(cm.astype(jnp.float32).reshape(b, nc, chunk, g, n), 1, 0)
    a_g = a.reshape(g, hg)
    mask = jnp.tril(jnp.ones((chunk, chunk), dtype=bool))[None, :, :, None, None]

    def step(h, inp):
        xk, dtk, bk, ck = inp
        acum = jnp.cumsum(dtk * a_g, axis=1)
        seg = acum[:, :, None] - acum[:, None, :]
        decay = jnp.exp(jnp.where(mask, seg, -jnp.inf))
        scores = jnp.einsum('blgn,bsgn->blsg', ck, bk)
        wts = scores[..., None] * decay * dtk[:, None]
        y = jnp.einsum('blsgh,bsghp->blghp', wts, xk)
        y = y + jnp.einsum('blgn,bghpn->blghp', ck, h) * jnp.exp(acum)[..., None]
        a_last = acum[:, -1]
        w_end = jnp.exp(a_last[:, None] - acum) * dtk
        h_new = jnp.exp(a_last)[..., None, None] * h + jnp.einsum('blgh,blgn,blghp->bghpn', w_end, bk, xk)
        return h_new, y

    h_init = h0.astype(jnp.float32).reshape(b, g, hg, p, n)
    h_fin, ys = lax.scan(step, h_init, (xc, dtc, bc, cc))
    y = jnp.moveaxis(ys, 0, 1).reshape(b, seq, nh, p)
    return y, h_fin.reshape(b, nh, p, n)


def ssd_mixer(h, conv_buf, ssm_state, w_in, conv_w, conv_b, dt_bias, a_log, d_skip, norm_g, w_out):
    b, seq, _ = h.shape
    proj = h @ w_in
    z, xbc, dt_raw = jnp.split(proj, [D_INNER, D_INNER + SSD_CONV_DIM], axis=-1)
    xbc, new_buf = causal_dwconv(xbc, conv_buf, conv_w, conv_b)
    xbc = jax.nn.silu(xbc)
    xs, bm, cm = jnp.split(xbc, [D_INNER, D_INNER + SSD_GROUPS * SSD_STATE], axis=-1)
    dt = jax.nn.softplus(dt_raw.astype(jnp.float32) + dt_bias.astype(jnp.float32))
    a = -jnp.exp(a_log.astype(jnp.float32))
    xh = xs.reshape(b, seq, SSD_HEADS, SSD_HEAD_DIM)
    chunk = min(CHUNK, seq)
    y, new_state = ssd_chunked(xh, dt, a,
                               bm.reshape(b, seq, SSD_GROUPS, SSD_STATE),
                               cm.reshape(b, seq, SSD_GROUPS, SSD_STATE), ssm_state, chunk)
    y = y + d_skip.astype(jnp.float32)[:, None] * xh.astype(jnp.float32)
    y = y.reshape(b, seq, D_INNER).astype(h.dtype)
    y = rms_norm(y * jax.nn.silu(z), norm_g)
    return y @ w_out, new_buf, new_state.astype(h.dtype)


def conv_module(h, buf, w_pw1, b_pw1, dw_w, dw_b, ln_g, ln_b, w_pw2, b_pw2):
    u = h @ w_pw1 + b_pw1
    ua, ug = jnp.split(u, 2, axis=-1)
    u = ua * jax.nn.sigmoid(ug)
    u, new_buf = causal_dwconv(u, buf, dw_w, dw_b)
    u = jax.nn.silu(layer_norm(u, ln_g, ln_b))
    return u @ w_pw2 + b_pw2, new_buf


def trunk(x, c, ssd_state, ssd_conv_cache, cmod_conv_cache,
          mod_w, mod_b, norm_g, ffn_w1, ffn_w3, ffn_w2,
          ssd_w_in, ssd_conv_w, ssd_conv_b, ssd_dt_bias, ssd_a_log, ssd_d, ssd_norm_g, ssd_w_out,
          cmod_w_pw1, cmod_b_pw1, cmod_dw_w, cmod_dw_b, cmod_ln_g, cmod_ln_b, cmod_w_pw2, cmod_b_pw2,
          final_g):
    b = x.shape[0]
    new_ssd_state, new_ssd_conv, new_cmod_conv = [], [], []
    c_act = jax.nn.silu(c)
    for i in range(DEPTH):
        mod = (c_act @ mod_w[i] + mod_b[i]).reshape(b, N_MOD, 1, D_MODEL)
        sh1, sc1, gt1, sh2, sc2, gt2, sh3, sc3, gt3 = [mod[:, k] for k in range(N_MOD)]
        h = rms_norm(x, norm_g[i, 0]) * (1 + sc1) + sh1
        x = x + 0.5 * gt1 * swiglu(h, ffn_w1[i, 0], ffn_w3[i, 0], ffn_w2[i, 0])
        h = rms_norm(x, norm_g[i, 1]) * (1 + sc2) + sh2
        j = i // N_MIXERS
        if i % N_MIXERS == 0:
            out, nb, ns = ssd_mixer(h, ssd_conv_cache[j], ssd_state[j], ssd_w_in[j], ssd_conv_w[j],
                                    ssd_conv_b[j], ssd_dt_bias[j], ssd_a_log[j], ssd_d[j],
                                    ssd_norm_g[j], ssd_w_out[j])
            new_ssd_conv.append(nb)
            new_ssd_state.append(ns)
        else:
            out, nb = conv_module(h, cmod_conv_cache[j], cmod_w_pw1[j], cmod_b_pw1[j], cmod_dw_w[j],
                                  cmod_dw_b[j], cmod_ln_g[j], cmod_ln_b[j], cmod_w_pw2[j], cmod_b_pw2[j])
            new_cmod_conv.append(nb)
        x = x + gt2 * out
        h = rms_norm(x, norm_g[i, 2]) * (1 + sc3) + sh3
        x = x + 0.5 * gt3 * swiglu(h, ffn_w1[i, 1], ffn_w3[i, 1], ffn_w2[i, 1])
    return rms_norm(x, final_g), jnp.stack(new_ssd_state), jnp.stack(new_ssd_conv), jnp.stack(new_cmod_conv)


def setup_inputs(seed: int = 0) -> dict:
    key = jax.random.key(seed)
    ks = iter(jax.random.split(key, 48))

    def nrm(shape, scale):
        return scale * jax.random.normal(next(ks), shape, jnp.float32)

    dt0 = jnp.exp(jax.random.uniform(next(ks), (N_SSD, SSD_HEADS), jnp.float32,
                                     minval=float(np.log(1e-3)), maxval=float(np.log(1e-1))))
    ssd_dt_bias = dt0 + jnp.log(-jnp.expm1(-dt0))
    ssd_a_log = jnp.log(jax.random.uniform(next(ks), (N_SSD, SSD_HEADS), jnp.float32, minval=1.0, maxval=16.0))
    return {
        'x_prompt': nrm((BATCH, SEQ, D_MODEL), 1.0),
        'x_sample': nrm((DEC_BATCH, DEC_SEQ, D_MODEL), 1.0),
        'c_prompt': nrm((BATCH, D_MODEL), 1.0),
        'c_sample': nrm((DEC_BATCH, D_MODEL), 1.0),
        'state_ssd': nrm((N_SSD, DEC_BATCH, SSD_HEADS, SSD_HEAD_DIM, SSD_STATE), 0.5),
        'cache_ssd_conv': nrm((N_SSD, DEC_BATCH, SSD_CONV_W - 1, SSD_CONV_DIM), 1.0),
        'cache_cmod_conv': nrm((N_CONV, DEC_BATCH, CONV_W - 1, D_CONV), 1.0),
        'mod_w': nrm((DEPTH, D_MODEL, N_MOD * D_MODEL), 0.5 * D_MODEL ** -0.5),
        'mod_b': nrm((DEPTH, N_MOD * D_MODEL), 0.02),
        'norm_g': 1.0 + nrm((DEPTH, 3, D_MODEL), 0.05),
        'ffn_w1': nrm((DEPTH, 2, D_MODEL, D_FF), D_MODEL ** -0.5),
        'ffn_w3': nrm((DEPTH, 2, D_MODEL, D_FF), D_MODEL ** -0.5),
        'ffn_w2': nrm((DEPTH, 2, D_FF, D_MODEL), D_FF ** -0.5),
        'ssd_w_in': nrm((N_SSD, D_MODEL, SSD_PROJ), D_MODEL ** -0.5),
        'ssd_conv_w': nrm((N_SSD, SSD_CONV_W, SSD_CONV_DIM), SSD_CONV_W ** -0.5),
        'ssd_conv_b': nrm((N_SSD, SSD_CONV_DIM), 0.02),
        'ssd_dt_bias': ssd_dt_bias,
        'ssd_a_log': ssd_a_log,
        'ssd_d': 1.0 + nrm((N_SSD, SSD_HEADS), 0.1),
        'ssd_norm_g': 1.0 + nrm((N_SSD, D_INNER), 0.05),
        'ssd_w_out': nrm((N_SSD, D_INNER, D_MODEL), D_INNER ** -0.5),
        'cmod_w_pw1': nrm((N_CONV, D_MODEL, 2 * D_CONV), D_MODEL ** -0.5),
        'cmod_b_pw1': nrm((N_CONV, 2 * D_CONV), 0.02),
        'cmod_dw_w': nrm((N_CONV, CONV_W, D_CONV), CONV_W ** -0.5),
        'cmod_dw_b': nrm((N_CONV, D_CONV), 0.02),
        'cmod_ln_g': 1.0 + nrm((N_CONV, D_CONV), 0.05),
        'cmod_ln_b': nrm((N_CONV, D_CONV), 0.02),
        'cmod_w_pw2': nrm((N_CONV, D_CONV, D_MODEL), D_CONV ** -0.5),
        'cmod_b_pw2': nrm((N_CONV, D_MODEL), 0.02),
        'final_g': 1.0 + nrm((D_MODEL,), 0.05),
    }


def reference(x_prompt, x_sample, c_prompt, c_sample, state_ssd, cache_ssd_conv, cache_cmod_conv,
              mod_w, mod_b, norm_g, ffn_w1, ffn_w3, ffn_w2,
              ssd_w_in, ssd_conv_w, ssd_conv_b, ssd_dt_bias, ssd_a_log, ssd_d, ssd_norm_g, ssd_w_out,
              cmod_w_pw1, cmod_b_pw1, cmod_dw_w, cmod_dw_b, cmod_ln_g, cmod_ln_b, cmod_w_pw2, cmod_b_pw2,
              final_g):
    weights = (mod_w, mod_b, norm_g, ffn_w1, ffn_w3, ffn_w2,
               ssd_w_in, ssd_conv_w, ssd_conv_b, ssd_dt_bias, ssd_a_log, ssd_d, ssd_norm_g, ssd_w_out,
               cmod_w_pw1, cmod_b_pw1, cmod_dw_w, cmod_dw_b, cmod_ln_g, cmod_ln_b, cmod_w_pw2, cmod_b_pw2,
               final_g)
    bp = x_prompt.shape[0]
    zero_state = jnp.zeros((N_SSD, bp) + state_ssd.shape[2:], x_prompt.dtype)
    zero_ssd_conv = jnp.zeros((N_SSD, bp) + cache_ssd_conv.shape[2:], x_prompt.dtype)
    zero_cmod_conv = jnp.zeros((N_CONV, bp) + cache_cmod_conv.shape[2:], x_prompt.dtype)
    y_prompt, st_p, sc_p, cc_p = trunk(x_prompt, c_prompt, zero_state, zero_ssd_conv, zero_cmod_conv, *weights)
    y_sample, st_s, sc_s, cc_s = trunk(x_sample, c_sample, state_ssd, cache_ssd_conv, cache_cmod_conv, *weights)
    return (y_prompt, y_sample, st_p, sc_p, cc_p, st_s, sc_s, cc_s)
```

```python
import functools

import jax
import jax.numpy as jnp
from jax import lax
from jax.experimental import pallas as pl
from jax.experimental.pallas import tpu as pltpu

F32 = jnp.float32
BF16 = jnp.bfloat16
EPS = 1e-6
N_MOD = 9
LANES = 128
SUBLANES = 8
SSD_CHUNK = 128
VMEM_LIMIT = 56 * 1024 * 1024


def _dot(a, b):
    return jnp.dot(a, b, preferred_element_type=F32)


def _silu(v):
    return v * jax.nn.sigmoid(v)


def _modulated_rms(x, g, sc, sh):
    y = x * lax.rsqrt(jnp.mean(x * x, axis=-1, keepdims=True) + EPS) * g
    return y * (1.0 + sc) + sh


def _split2(v):
    hi = v.astype(BF16)
    lo = (v - hi.astype(F32)).astype(BF16)
    return jnp.concatenate([hi, lo], axis=1)


def _round16(v):
    hi = v.astype(BF16)
    lo = (v - hi.astype(F32)).astype(BF16)
    return hi.astype(F32) + lo.astype(F32)


def _resident(shape, index_map):
    return pl.BlockSpec(shape, index_map, pipeline_mode=pl.Buffered(1))


def _mod_kernel(c_ref, w_ref, b_ref, o_ref):
    c = c_ref[...]
    o_ref[...] = _dot(_silu(c).astype(BF16), w_ref[...].astype(BF16)) + b_ref[...]


def _mod_call(c_all, mod_w, mod_b):
    depth, d, nm = mod_w.shape
    rows = c_all.shape[0]
    tn = 1152
    assert nm % tn == 0
    return pl.pallas_call(
        _mod_kernel,
        grid=(depth, nm // tn),
        in_specs=[
            pl.BlockSpec((rows, d), lambda i, j: (0, 0)),
            pl.BlockSpec((None, d, tn), lambda i, j: (i, 0, j)),
            pl.BlockSpec((None, 1, tn), lambda i, j: (i, 0, j)),
        ],
        out_specs=pl.BlockSpec((None, rows, tn), lambda i, j: (i, 0, j)),
        out_shape=jax.ShapeDtypeStruct((depth, rows, nm), F32),
        compiler_params=pltpu.CompilerParams(
            dimension_semantics=("arbitrary", "arbitrary"), vmem_limit_bytes=VMEM_LIMIT),
        name="adaln_mod",
    )(c_all, mod_w, mod_b.reshape(depth, 1, nm))


def _ffn_kernel(x_ref, mod_ref, g_ref, w1_ref, w3_ref, w2_ref, *rest, mod_row, n_chunks, final):
    if final:
        fg_ref, o_ref, hid_ref = rest
    else:
        o_ref, hid_ref = rest
    x = x_ref[...]
    sh = mod_ref[mod_row:mod_row + 1, :]
    sc = mod_ref[mod_row + 1:mod_row + 2, :]
    gt = mod_ref[mod_row + 2:mod_row + 3, :]
    hb = _modulated_rms(x, g_ref[...], sc, sh).astype(BF16)
    fc = w1_ref.shape[1] // n_chunks
    for c in range(n_chunks):
        a = _dot(hb, w1_ref[:, c * fc:(c + 1) * fc])
        b = _dot(hb, w3_ref[:, c * fc:(c + 1) * fc])
        hid_ref[:, c * fc:(c + 1) * fc] = (_silu(a) * b).astype(BF16)
    y = x + (0.5 * gt) * _dot(hid_ref[...], w2_ref[...])
    if final:
        y = y * lax.rsqrt(jnp.mean(y * y, axis=-1, keepdims=True) + EPS) * fg_ref[...]
    o_ref[...] = y


def _ffn_call(x, mod, layer, boff, mod_row, norm_g, gi, w1, w3, w2, li, lj, tm, final_g=None):
    b, s, d = x.shape
    dff = w1.shape[-1]
    n_chunks = 2
    assert s % tm == 0 and dff % (n_chunks * LANES) == 0
    in_specs = [
        pl.BlockSpec((None, tm, d), lambda i, j: (i, j, 0)),
        pl.BlockSpec((None, None, N_MOD, d), lambda i, j: (layer, i + boff, 0, 0)),
        pl.BlockSpec((None, 1, d), lambda i, j: (gi, 0, 0)),
        _resident((None, None, d, dff), lambda i, j: (li, lj, 0, 0)),
        _resident((None, None, d, dff), lambda i, j: (li, lj, 0, 0)),
        _resident((None, None, dff, d), lambda i, j: (li, lj, 0, 0)),
    ]
    args = [x, mod, norm_g, w1, w3, w2]
    if final_g is not None:
        in_specs.append(pl.BlockSpec((1, d), lambda i, j: (0, 0)))
        args.append(final_g)
    return pl.pallas_call(
        functools.partial(_ffn_kernel, mod_row=mod_row, n_chunks=n_chunks, final=final_g is not None),
        grid=(b, s // tm),
        in_specs=in_specs,
        out_specs=pl.BlockSpec((None, tm, d), lambda i, j: (i, j, 0)),
        out_shape=jax.ShapeDtypeStruct((b, s, d), F32),
        scratch_shapes=[pltpu.VMEM((tm, dff), BF16)],
        compiler_params=pltpu.CompilerParams(
            dimension_semantics=("arbitrary", "arbitrary"), vmem_limit_bytes=VMEM_LIMIT),
        name="swiglu_ffn",
    )(*args)


def _dwconv(ext_ref, w_ref, b_ref, taps, first_row, rows, cols, emit, rb=32, cb=256):
    for c0 in range(0, cols, cb):
        for r0 in range(0, rows, rb):
            acc = jnp.broadcast_to(b_ref[:, c0:c0 + cb], (rb, cb))
            for k in range(taps):
                lo = first_row + k + r0
                acc = acc + w_ref[k:k + 1, c0:c0 + cb] * ext_ref[lo:lo + rb, c0:c0 + cb]
            emit(r0, c0, acc)


def _cmod_kernel(x_ref, mod_ref, g_ref, wp1_ref, bp1_ref, dww_ref, dwb_ref, lng_ref, lnb_ref,
                 wp2_ref, bp2_ref, cache0_ref, o_ref, cache_ref, ext_ref, v_ref, *, tm, valid, pad):
    s = pl.program_id(1)
    taps = dww_ref.shape[0]
    dc = dww_ref.shape[1]

    @pl.when(s == 0)
    def _():
        ext_ref[0:pad, :] = cache0_ref[...]

    x = x_ref[...]
    sh = mod_ref[3:4, :]
    sc = mod_ref[4:5, :]
    gt = mod_ref[5:6, :]
    hb = _modulated_rms(x, g_ref[...], sc, sh).astype(BF16)
    u = _dot(hb, wp1_ref[...]) + bp1_ref[...]
    ext_ref[pad:pad + tm, :] = u[:, :dc] * jax.nn.sigmoid(u[:, dc:])
    cache_ref[...] = ext_ref[pad + valid - (taps - 1):pad + valid, :]

    def emit(r0, c0, acc):
        v_ref[r0:r0 + acc.shape[0], c0:c0 + acc.shape[1]] = acc

    _dwconv(ext_ref, dww_ref, dwb_ref, taps, pad - (taps - 1), tm, dc, emit)
    ext_ref[0:pad, :] = ext_ref[tm:tm + pad, :]

    v = v_ref[...]
    vc = v - jnp.mean(v, axis=-1, keepdims=True)
    y = vc * lax.rsqrt(jnp.mean(vc * vc, axis=-1, keepdims=True) + EPS) * lng_ref[...] + lnb_ref[...]
    out = _dot(_silu(y).astype(BF16), wp2_ref[...]) + bp2_ref[...]
    o_ref[...] = x + gt * out


def _cmod_call(x, mod, layer, boff, norm_g, gi, wp1, bp1, dww, dwb, lng, lnb, wp2, bp2, cache0, j, tm, valid):
    b, s, d = x.shape
    taps, dc = dww.shape[-2:]
    pad = 32
    assert s % tm == 0 and taps - 1 <= pad and (valid == tm or s == tm)
    row = lambda a: a.reshape(a.shape[0], 1, a.shape[-1])
    out, cache = pl.pallas_call(
        functools.partial(_cmod_kernel, tm=tm, valid=valid, pad=pad),
        grid=(b, s // tm),
        in_specs=[
            pl.BlockSpec((None, tm, d), lambda i, k: (i, k, 0)),
            pl.BlockSpec((None, None, N_MOD, d), lambda i, k: (layer, i + boff, 0, 0)),
            pl.BlockSpec((None, 1, d), lambda i, k: (gi, 0, 0)),
            _resident((None, d, 2 * dc), lambda i, k: (j, 0, 0)),
            pl.BlockSpec((None, 1, 2 * dc), lambda i, k: (j, 0, 0)),
            pl.BlockSpec((None, taps, dc), lambda i, k: (j, 0, 0)),
            pl.BlockSpec((None, 1, dc), lambda i, k: (j, 0, 0)),
            pl.BlockSpec((None, 1, dc), lambda i, k: (j, 0, 0)),
            pl.BlockSpec((None, 1, dc), lambda i, k: (j, 0, 0)),
            _resident((None, dc, d), lambda i, k: (j, 0, 0)),
            pl.BlockSpec((None, 1, d), lambda i, k: (j, 0, 0)),
            pl.BlockSpec((None, pad, dc), lambda i, k: (i, 0, 0)),
        ],
        out_specs=[
            pl.BlockSpec((None, tm, d), lambda i, k: (i, k, 0)),
            pl.BlockSpec((None, taps - 1, dc), lambda i, k: (i, 0, 0)),
        ],
        out_shape=[
            jax.ShapeDtypeStruct((b, s, d), F32),
            jax.ShapeDtypeStruct((b, taps - 1, dc), F32),
        ],
        scratch_shapes=[pltpu.VMEM((tm + pad, dc), F32), pltpu.VMEM((tm, dc), F32)],
        compiler_params=pltpu.CompilerParams(
            dimension_semantics=("arbitrary", "arbitrary"), vmem_limit_bytes=VMEM_LIMIT),
        name="conformer_conv",
    )(x, mod, norm_g, wp1, row(bp1), dww, row(dwb), row(lng), row(lnb), wp2, row(bp2), cache0)
    return out, cache


def _ssd_kernel(x_ref, mod_ref, g_ref, win_ref, cw_ref, cb_ref, dtb_ref, alog_ref, dexp_ref, ng_ref,
                wout_ref, ehp_ref, ehs_ref, cache0_ref, st0_ref,
                o_ref, cache_ref, st_ref,
                z_ref, ext_ref, act_ref, y_ref, *, tm, valid, di, gn, n, heads):
    s = pl.program_id(1)
    taps = cw_ref.shape[0]
    cd = cw_ref.shape[1]
    pad = SUBLANES
    lc = SSD_CHUNK
    groups = gn // n
    hg = heads // groups
    p = di // heads
    gw = hg * p

    @pl.when(s == 0)
    def _():
        ext_ref[0:pad, :] = cache0_ref[...]
        st_ref[...] = st0_ref[...]

    x = x_ref[...]
    sh = mod_ref[3:4, :]
    sc = mod_ref[4:5, :]
    gt = mod_ref[5:6, :]
    hb = _modulated_rms(x, g_ref[...], sc, sh).astype(BF16)
    z_ref[...] = _dot(hb, win_ref[:, 0:di])
    ext_ref[pad:pad + tm, :] = _dot(hb, win_ref[:, di:di + cd])
    dt_raw = _dot(hb, win_ref[:, di + cd:di + cd + LANES])
    cache_ref[...] = ext_ref[pad + valid - (taps - 1):pad + valid, :]

    def emit(r0, c0, acc):
        act_ref[r0:r0 + acc.shape[0], c0:c0 + acc.shape[1]] = _silu(acc)

    _dwconv(ext_ref, cw_ref, cb_ref, taps, pad - (taps - 1), tm, cd, emit)
    ext_ref[0:pad, :] = ext_ref[tm:tm + pad, :]

    v = dt_raw + dtb_ref[...]
    dt = jnp.maximum(v, 0.0) + jnp.log1p(jnp.exp(-jnp.abs(v)))
    if valid < tm:
        dt = jnp.where(lax.broadcasted_iota(jnp.int32, dt.shape, 0) < valid, dt, 0.0)
    da = dt * (-jnp.exp(alog_ref[...]))

    ri = lax.broadcasted_iota(jnp.int32, (tm, tm), 0)
    ci = lax.broadcasted_iota(jnp.int32, (tm, tm), 1)
    tril = jnp.where((ri >= ci) & (ri // lc == ci // lc), 1.0, 0.0).astype(BF16)
    d_hi = da.astype(BF16)
    d_r = da - d_hi.astype(F32)
    d_mid = d_r.astype(BF16)
    d_lo = (d_r - d_mid.astype(F32)).astype(BF16)
    cs = _dot(tril, jnp.concatenate([d_hi, d_mid, d_lo], axis=1))
    acum = _round16(cs[:, 0:LANES] + cs[:, LANES:2 * LANES] + cs[:, 2 * LANES:3 * LANES])
    acum_t = acum.T

    causal = (lax.broadcasted_iota(jnp.int32, (lc, lc), 0) >= lax.broadcasted_iota(jnp.int32, (lc, lc), 1))
    left = lax.broadcasted_iota(jnp.int32, (lc, 2 * p), 1) < p

    for c in range(tm // lc):
        r0 = c * lc
        ac = acum[r0:r0 + lc, :]
        al = ac[lc - 1:lc, :]
        stack = jnp.concatenate(
            [dt[r0:r0 + lc, :], jnp.exp(ac), jnp.exp(al - ac), jnp.broadcast_to(jnp.exp(al), (SUBLANES, LANES))],
            axis=0)
        ex = _dot(_split2(stack), ehp_ref[...])
        dt_x = ex[0:lc, :]
        eac_x = ex[lc:2 * lc, :]
        wend_x = ex[2 * lc:3 * lc, :]
        dec_x = ex[3 * lc:3 * lc + 1, :]
        xs = act_ref[r0:r0 + lc, 0:di]
        xdt = xs * dt_x
        xw = (xdt * wend_x).astype(BF16)
        row_t = acum_t[:, r0:r0 + lc]
        ac2 = _split2(ac)
        y_parts = []
        for g in range(groups):
            bg = act_ref[r0:r0 + lc, di + g * n:di + (g + 1) * n]
            cgb = act_ref[r0:r0 + lc, di + gn + g * n:di + gn + (g + 1) * n].astype(BF16)
            scores = lax.dot_general(cgb, bg.astype(BF16), (((1,), (1,)), ((), ())),
                                     preferred_element_type=F32)
            h_g = st_ref[:, g * gw:(g + 1) * gw]
            y_int = _dot(cgb, h_g.astype(BF16))
            colb = _dot(ac2, ehs_ref[:, g * hg * lc:(g + 1) * hg * lc])
            ys = []
            for jp in range(hg // 2):
                wl = []
                for j in (2 * jp, 2 * jp + 1):
                    hh = g * hg + j
                    seg = colb[:, j * lc:(j + 1) * lc] - row_t[hh:hh + 1, :]
                    decay = jnp.exp(jnp.where(causal, seg, -jnp.inf))
                    wl.append((scores * decay).astype(BF16))
                w2 = jnp.concatenate(wl, axis=1)
                xp = xdt[:, (g * hg + 2 * jp) * p:(g * hg + 2 * jp + 2) * p]
                x2 = jnp.concatenate([jnp.where(left, xp, 0.0), jnp.where(left, 0.0, xp)],
                                     axis=0).astype(BF16)
                ys.append(_dot(w2, x2))
            y_parts.append(jnp.concatenate(ys, axis=1) + y_int * eac_x[:, g * gw:(g + 1) * gw])
            st_ref[:, g * gw:(g + 1) * gw] = (h_g * dec_x[:, g * gw:(g + 1) * gw]
                                              + _dot(bg.T.astype(BF16), xw[:, g * gw:(g + 1) * gw]))
        y = jnp.concatenate(y_parts, axis=1) + dexp_ref[...] * xs
        y_ref[r0:r0 + lc, :] = y * _silu(z_ref[r0:r0 + lc, :])

    yv = y_ref[...]
    yn = yv * lax.rsqrt(jnp.mean(yv * yv, axis=-1, keepdims=True) + EPS) * ng_ref[...]
    o_ref[...] = x + gt * _dot(yn.astype(BF16), wout_ref[...])


def _ssd_call(x, mod, layer, boff, norm_g, gi, win, cw, cb, dtb, alog, dexp, ng, wout, ehp, ehs,
              cache0, st0, j, tm, valid, heads, n):
    b, s, d = x.shape
    taps, cd = cw.shape[-2:]
    di = wout.shape[-2]
    gn = (cd - di) // 2
    pw = win.shape[-1]
    assert s % tm == 0 and tm % SSD_CHUNK == 0 and (valid == tm or s == tm)
    row = lambda a: a.reshape(a.shape[0], 1, a.shape[-1])
    return pl.pallas_call(
        functools.partial(_ssd_kernel, tm=tm, valid=valid, di=di, gn=gn, n=n, heads=heads),
        grid=(b, s // tm),
        in_specs=[
            pl.BlockSpec((None, tm, d), lambda i, k: (i, k, 0)),
            pl.BlockSpec((None, None, N_MOD, d), lambda i, k: (layer, i + boff, 0, 0)),
            pl.BlockSpec((None, 1, d), lambda i, k: (gi, 0, 0)),
            _resident((None, d, pw), lambda i, k: (j, 0, 0)),
            pl.BlockSpec((None, taps, cd), lambda i, k: (j, 0, 0)),
            pl.BlockSpec((None, 1, cd), lambda i, k: (j, 0, 0)),
            pl.BlockSpec((None, 1, LANES), lambda i, k: (j, 0, 0)),
            pl.BlockSpec((None, 1, LANES), lambda i, k: (j, 0, 0)),
            pl.BlockSpec((None, 1, di), lambda i, k: (j, 0, 0)),
            pl.BlockSpec((None, 1, di), lambda i, k: (j, 0, 0)),
            _resident((None, di, d), lambda i, k: (j, 0, 0)),
            _resident(ehp.shape, lambda i, k: (0, 0)),
            _resident(ehs.shape, lambda i, k: (0, 0)),
            pl.BlockSpec((None, SUBLANES, cd), lambda i, k: (i, 0, 0)),
            pl.BlockSpec((None, n, di), lambda i, k: (i, 0, 0)),
        ],
        out_specs=[
            pl.BlockSpec((None, tm, d), lambda i, k: (i, k, 0)),
            pl.BlockSpec((None, taps - 1, cd), lambda i, k: (i, 0, 0)),
            pl.BlockSpec((None, n, di), lambda i, k: (i, 0, 0)),
        ],
        out_shape=[
            jax.ShapeDtypeStruct((b, s, d), F32),
            jax.ShapeDtypeStruct((b, taps - 1, cd), F32),
            jax.ShapeDtypeStruct((b, n, di), F32),
        ],
        scratch_shapes=[
            pltpu.VMEM((tm, di), F32),
            pltpu.VMEM((tm + SUBLANES, cd), F32),
            pltpu.VMEM((tm, cd), F32),
            pltpu.VMEM((tm, di), F32),
        ],
        compiler_params=pltpu.CompilerParams(
            dimension_semantics=("arbitrary", "arbitrary"), vmem_limit_bytes=VMEM_LIMIT),
        name="ssd_mixer",
    )(x, mod, norm_g, win, cw, row(cb), row(dtb), row(alog), row(dexp), row(ng), wout, ehp, ehs, cache0, st0)


def _head_expanders(heads, p, lc):
    h_of_row = jnp.arange(2 * LANES) % LANES
    ehp = (h_of_row[:, None] == (jnp.arange(heads * p) // p)[None, :]).astype(BF16)
    ehs = (h_of_row[:, None] == (jnp.arange(heads * lc) // lc)[None, :]).astype(BF16)
    return ehp, ehs


def _trunk(x, mod, boff, valid, tiles, ssd_state_t, ssd_conv_cache, cmod_conv_cache, wts):
    (norm_g, w1, w3, w2, win, cw, cb, dtb, alog, dexp, ssd_ng, wout, ehp, ehs,
     wp1, bp1, dww, dwb, lng, lnb, wp2, bp2, final_g, heads, n) = wts
    depth = norm_g.shape[0] // 3
    tm_ffn, tm_ssd, tm_conv = tiles
    new_state, new_ssd_conv, new_cmod_conv = [], [], []
    for i in range(depth):
        x = _ffn_call(x, mod, i, boff, 0, norm_g, 3 * i, w1, w3, w2, i, 0, tm_ffn)
        j = i // 2
        if i % 2 == 0:
            x, cache, st = _ssd_call(x, mod, i, boff, norm_g, 3 * i + 1, win, cw, cb, dtb, alog, dexp, ssd_ng,
                                     wout, ehp, ehs, ssd_conv_cache[j], ssd_state_t[j], j, tm_ssd,
                                     min(valid, tm_ssd), heads, n)
            new_ssd_conv.append(cache)
            new_state.append(st)
        else:
            x, cache = _cmod_call(x, mod, i, boff, norm_g, 3 * i + 1, wp1, bp1, dww, dwb, lng, lnb, wp2, bp2,
                                  cmod_conv_cache[j], j, tm_conv, min(valid, tm_conv))
            new_cmod_conv.append(cache)
        x = _ffn_call(x, mod, i, boff, 6, norm_g, 3 * i + 2, w1, w3, w2, i, 1, tm_ffn,
                      final_g=final_g if i == depth - 1 else None)
    return x, jnp.stack(new_state), jnp.stack(new_ssd_conv), jnp.stack(new_cmod_conv)


def kernel(x_prompt, x_sample, c_prompt, c_sample, state_ssd, cache_ssd_conv, cache_cmod_conv, mod_w, mod_b, norm_g, ffn_w1, ffn_w3, ffn_w2, ssd_w_in, ssd_conv_w, ssd_conv_b, ssd_dt_bias, ssd_a_log, ssd_d, ssd_norm_g, ssd_w_out, cmod_w_pw1, cmod_b_pw1, cmod_dw_w, cmod_dw_b, cmod_ln_g, cmod_ln_b, cmod_w_pw2, cmod_b_pw2, final_g):
    bp, sp, d = x_prompt.shape
    bs, ss, _ = x_sample.shape
    n_ssd, _, heads, p, n = state_ssd.shape
    di = heads * p
    cd = ssd_conv_w.shape[-1]
    depth = mod_w.shape[0]
    assert heads <= LANES and ss <= SSD_CHUNK

    mod = _mod_call(jnp.concatenate([c_prompt, c_sample], axis=0), mod_w, mod_b)
    mod = mod.reshape(depth, bp + bs, N_MOD, d)

    lane_pad = lambda a: jnp.pad(a, ((0, 0), (0, LANES - a.shape[-1])))
    win = jnp.concatenate(
        [ssd_w_in[..., :di + cd], jnp.pad(ssd_w_in[..., di + cd:], ((0, 0), (0, 0), (0, LANES - heads)))],
        axis=-1).astype(BF16)
    ehp, ehs = _head_expanders(heads, p, SSD_CHUNK)
    wts = (norm_g.reshape(depth * 3, 1, d), ffn_w1.astype(BF16), ffn_w3.astype(BF16), ffn_w2.astype(BF16),
           win, ssd_conv_w, ssd_conv_b, lane_pad(ssd_dt_bias), lane_pad(ssd_a_log),
           jnp.repeat(ssd_d, p, axis=-1), ssd_norm_g, ssd_w_out.astype(BF16), ehp, ehs,
           cmod_w_pw1.astype(BF16), cmod_b_pw1, cmod_dw_w, cmod_dw_b, cmod_ln_g, cmod_ln_b,
           cmod_w_pw2.astype(BF16), cmod_b_pw2, final_g.reshape(1, d), heads, n)

    def to_t(st):
        return jnp.swapaxes(st.reshape(st.shape[0], st.shape[1], di, n), 2, 3)

    def from_t(st_t):
        return jnp.swapaxes(st_t, 2, 3).reshape(st_t.shape[0], st_t.shape[1], heads, p, n)

    cmod_pad = 32 - cache_cmod_conv.shape[2]
    ssd_pad = SUBLANES - cache_ssd_conv.shape[2]

    zeros = lambda *shape: jnp.zeros(shape, F32)
    y_p, st_p, sc_p, cc_p = _trunk(
        x_prompt, mod, 0, sp, (512, 256, 256),
        zeros(n_ssd, bp, n, di), zeros(n_ssd, bp, SUBLANES, cd),
        zeros(cache_cmod_conv.shape[0], bp, 32, cache_cmod_conv.shape[-1]), wts)

    xs_pad = jnp.pad(x_sample, ((0, 0), (0, SSD_CHUNK - ss), (0, 0)))
    y_s, st_s, sc_s, cc_s = _trunk(
        xs_pad, mod, bp, ss, (SSD_CHUNK, SSD_CHUNK, SSD_CHUNK),
        to_t(state_ssd), jnp.pad(cache_ssd_conv, ((0, 0), (0, 0), (ssd_pad, 0), (0, 0))),
        jnp.pad(cache_cmod_conv, ((0, 0), (0, 0), (cmod_pad, 0), (0, 0))), wts)

    return (y_p, y_s[:, :ss], from_t(st_p), sc_p, cc_p, from_t(st_s), sc_s, cc_s)
```

```python
import functools

import jax
import jax.numpy as jnp
from jax import lax
from jax.experimental import pallas as pl
from jax.experimental.pallas import tpu as pltpu

F32 = jnp.float32
BF16 = jnp.bfloat16
EPS = 1e-6
LOG2E = 1.4426950408889634
N_MOD = 9
LANES = 128
SUBLANES = 8
PACKED_ROWS = 16
SSD_CHUNK = 128
VMEM_LIMIT = 56 * 1024 * 1024


def _dot(a, b):
    return jnp.dot(a, b, preferred_element_type=F32)


def _silu(v):
    return v * jax.nn.sigmoid(v)


def _modulated_rms(x, g, sc, sh):
    y = x * lax.rsqrt(jnp.mean(x * x, axis=-1, keepdims=True) + EPS) * g
    return y * (1.0 + sc) + sh


def _split2(v):
    hi = v.astype(BF16)
    lo = (v - hi.astype(F32)).astype(BF16)
    return jnp.concatenate([hi, lo], axis=1)


def _round16(v):
    hi = v.astype(BF16)
    lo = (v - hi.astype(F32)).astype(BF16)
    return hi.astype(F32) + lo.astype(F32)


def _resident(shape, index_map):
    return pl.BlockSpec(shape, index_map, pipeline_mode=pl.Buffered(1))


def _mod_kernel(c_ref, w_ref, b_ref, o_ref):
    c = c_ref[...]
    o_ref[...] = _dot(_silu(c).astype(BF16), w_ref[...].astype(BF16)) + b_ref[...]


def _mod_call(c_all, mod_w, mod_b):
    depth, d, nm = mod_w.shape
    rows = c_all.shape[0]
    tn = 1152
    assert nm % tn == 0
    return pl.pallas_call(
        _mod_kernel,
        grid=(depth, nm // tn),
        in_specs=[
            pl.BlockSpec((rows, d), lambda i, j: (0, 0)),
            pl.BlockSpec((None, d, tn), lambda i, j: (i, 0, j)),
            pl.BlockSpec((None, 1, tn), lambda i, j: (i, 0, j)),
        ],
        out_specs=pl.BlockSpec((None, rows, tn), lambda i, j: (i, 0, j)),
        out_shape=jax.ShapeDtypeStruct((depth, rows, nm), F32),
        compiler_params=pltpu.CompilerParams(
            dimension_semantics=("arbitrary", "arbitrary"), vmem_limit_bytes=VMEM_LIMIT),
        name="adaln_mod",
    )(c_all, mod_w, mod_b.reshape(depth, 1, nm))


def _ffn_kernel(x_ref, mod_ref, g_ref, w1_ref, w3_ref, w2_ref, *rest, mod_row, n_chunks, final):
    if final:
        fg_ref, o_ref, hid_ref = rest
    else:
        o_ref, hid_ref = rest
    x = x_ref[...]
    sh = mod_ref[mod_row:mod_row + 1, :]
    sc = mod_ref[mod_row + 1:mod_row + 2, :]
    gt = mod_ref[mod_row + 2:mod_row + 3, :]
    hb = _modulated_rms(x, g_ref[...], sc, sh).astype(BF16)
    fc = w1_ref.shape[1] // n_chunks
    for c in range(n_chunks):
        a = _dot(hb, w1_ref[:, c * fc:(c + 1) * fc])
        b = _dot(hb, w3_ref[:, c * fc:(c + 1) * fc])
        hid_ref[:, c * fc:(c + 1) * fc] = (_silu(a) * b).astype(BF16)
    y = x + (0.5 * gt) * _dot(hid_ref[...], w2_ref[...])
    if final:
        y = y * lax.rsqrt(jnp.mean(y * y, axis=-1, keepdims=True) + EPS) * fg_ref[...]
    o_ref[...] = y


def _ffn_call(x, mod, layer, boff, mod_row, norm_g, gi, w1, w3, w2, li, lj, tm, final_g=None):
    b, s, d = x.shape
    dff = w1.shape[-1]
    n_chunks = 2
    assert s % tm == 0 and dff % (n_chunks * LANES) == 0
    in_specs = [
        pl.BlockSpec((None, tm, d), lambda i, j: (i, j, 0)),
        pl.BlockSpec((None, None, N_MOD, d), lambda i, j: (layer, i + boff, 0, 0)),
        pl.BlockSpec((None, 1, d), lambda i, j: (gi, 0, 0)),
        _resident((None, None, d, dff), lambda i, j: (li, lj, 0, 0)),
        _resident((None, None, d, dff), lambda i, j: (li, lj, 0, 0)),
        _resident((None, None, dff, d), lambda i, j: (li, lj, 0, 0)),
    ]
    args = [x, mod, norm_g, w1, w3, w2]
    if final_g is not None:
        in_specs.append(pl.BlockSpec((1, d), lambda i, j: (0, 0)))
        args.append(final_g)
    return pl.pallas_call(
        functools.partial(_ffn_kernel, mod_row=mod_row, n_chunks=n_chunks, final=final_g is not None),
        grid=(b, s // tm),
        in_specs=in_specs,
        out_specs=pl.BlockSpec((None, tm, d), lambda i, j: (i, j, 0)),
        out_shape=jax.ShapeDtypeStruct((b, s, d), F32),
        scratch_shapes=[pltpu.VMEM((tm, dff), BF16)],
        compiler_params=pltpu.CompilerParams(
            dimension_semantics=("arbitrary", "arbitrary"), vmem_limit_bytes=VMEM_LIMIT),
        name="swiglu_ffn",
    )(*args)


CONV_ROWS = 64


def _shift_rows(e, r):
    return pltpu.roll(e, e.shape[0] - r, axis=0)


def _aligned(i):
    return pl.multiple_of(i, SUBLANES)


def _cmod_kernel(x_ref, mod_ref, g_ref, wp1_ref, bp1_ref, dww_ref, dwb_ref, lng_ref, lnb_ref,
                 wp2_ref, bp2_ref, cache0_ref, o_ref, cache_ref, hb_ref, ext_ref, sh_ref, v_ref, *, tm, valid, pad):
    s = pl.program_id(1)
    taps = dww_ref.shape[0]
    dc = dww_ref.shape[1] * LANES
    pk = PACKED_ROWS

    @pl.when(s == 0)
    def _():
        ext_ref[0:pad, :] = cache0_ref[...]

    x = x_ref[...]
    sh = mod_ref[3:4, :]
    sc = mod_ref[4:5, :]
    gt = mod_ref[5:6, :]
    hb_ref[...] = _modulated_rms(x, g_ref[...], sc, sh).astype(BF16)
    keep = sh_ref.shape[2] - pk
    first = pad - (taps - 1)
    nq = CONV_ROWS // pk
    slab = 2 * LANES

    def glu_slab(s0):
        hb = hb_ref[...]
        ua = _dot(hb, wp1_ref[:, s0:s0 + slab]) + bp1_ref[:, s0:s0 + slab]
        ug = _dot(hb, wp1_ref[:, dc + s0:dc + s0 + slab]) + bp1_ref[:, dc + s0:dc + s0 + slab]
        ext_ref[pad:pad + tm, s0:s0 + slab] = ua * jax.nn.sigmoid(ug)

    def conv_slab(s0):
        for c0 in range(s0, s0 + slab, LANES):
            ct = c0 // LANES
            e = ext_ref[:, c0:c0 + LANES]
            for r in range(SUBLANES):
                er = e if r == 0 else _shift_rows(e, r)
                sh_ref[r, ct, :, :] = er.astype(BF16)
                sh_ref[r + SUBLANES, ct, 0:keep, :] = er[SUBLANES:SUBLANES + keep, :].astype(BF16)
            for r0 in range(0, tm, CONV_ROWS):
                accs = [None] * nq
                for k in range(taps):
                    a, r = divmod(first + k, pk)
                    wk = dww_ref[k, ct].astype(F32)
                    for q in range(nq):
                        lo = r0 + pk * (a + q)
                        term = wk * sh_ref[r, ct, lo:lo + pk, :].astype(F32)
                        accs[q] = term if accs[q] is None else accs[q] + term
                for q in range(nq):
                    v_ref[r0 + pk * q:r0 + pk * (q + 1), c0:c0 + LANES] = accs[q] + dwb_ref[:, c0:c0 + LANES]

    for s0 in range(0, dc, slab):
        glu_slab(s0)
    for s0 in range(0, dc, slab):
        @pl.when(s >= 0)
        def _():
            conv_slab(s0)

    cache_ref[...] = ext_ref[pad + valid - (taps - 1):pad + valid, :]
    ext_ref[0:pad, :] = ext_ref[tm:tm + pad, :]

    v = v_ref[...]
    vc = v - jnp.mean(v, axis=-1, keepdims=True)
    y = vc * lax.rsqrt(jnp.mean(vc * vc, axis=-1, keepdims=True) + EPS) * lng_ref[...] + lnb_ref[...]
    out = _dot(_silu(y).astype(BF16), wp2_ref[...]) + bp2_ref[...]
    o_ref[...] = x_ref[...] + gt * out


def _cmod_call(x, mod, layer, boff, norm_g, gi, wp1, bp1, dww, dwb, lng, lnb, wp2, bp2, cache0, j, tm, valid):
    b, s, d = x.shape
    taps, dc = dww.shape[-2:]
    pad = 32
    assert s % tm == 0 and tm % CONV_ROWS == 0 and taps - 1 <= pad and (valid == tm or s == tm)
    assert dc % LANES == 0
    row = lambda a: a.reshape(a.shape[0], 1, a.shape[-1])
    sub = lambda a: jnp.broadcast_to(a[..., None, :], a.shape[:-1] + (PACKED_ROWS, a.shape[-1]))
    dww_t = jnp.broadcast_to(dww.astype(BF16).reshape(dww.shape[0], taps, dc // LANES, 1, LANES),
                             (dww.shape[0], taps, dc // LANES, PACKED_ROWS, LANES))
    out, cache = pl.pallas_call(
        functools.partial(_cmod_kernel, tm=tm, valid=valid, pad=pad),
        grid=(b, s // tm),
        in_specs=[
            pl.BlockSpec((None, tm, d), lambda i, k: (i, k, 0)),
            pl.BlockSpec((None, None, N_MOD, d), lambda i, k: (layer, i + boff, 0, 0)),
            pl.BlockSpec((None, 1, d), lambda i, k: (gi, 0, 0)),
            _resident((None, d, 2 * dc), lambda i, k: (j, 0, 0)),
            pl.BlockSpec((None, 1, 2 * dc), lambda i, k: (j, 0, 0)),
            pl.BlockSpec((None, taps, dc // LANES, PACKED_ROWS, LANES), lambda i, k: (j, 0, 0, 0, 0)),
            pl.BlockSpec((None, PACKED_ROWS, dc), lambda i, k: (j, 0, 0)),
            pl.BlockSpec((None, 1, dc), lambda i, k: (j, 0, 0)),
            pl.BlockSpec((None, 1, dc), lambda i, k: (j, 0, 0)),
            _resident((None, dc, d), lambda i, k: (j, 0, 0)),
            pl.BlockSpec((None, 1, d), lambda i, k: (j, 0, 0)),
            pl.BlockSpec((None, pad, dc), lambda i, k: (i, 0, 0)),
        ],
        out_specs=[
            pl.BlockSpec((None, tm, d), lambda i, k: (i, k, 0)),
            pl.BlockSpec((None, taps - 1, dc), lambda i, k: (i, 0, 0)),
        ],
        out_shape=[
            jax.ShapeDtypeStruct((b, s, d), F32),
            jax.ShapeDtypeStruct((b, taps - 1, dc), F32),
        ],
        scratch_shapes=[
            pltpu.VMEM((tm, d), BF16),
            pltpu.VMEM((tm + pad, dc), F32),
            pltpu.VMEM((PACKED_ROWS, dc // LANES, tm + pad, LANES), BF16),
            pltpu.VMEM((tm, dc), F32),
        ],
        compiler_params=pltpu.CompilerParams(
            dimension_semantics=("arbitrary", "arbitrary"), vmem_limit_bytes=VMEM_LIMIT),
        name="conformer_conv",
    )(x, mod, norm_g, wp1, row(bp1), dww_t, sub(dwb), row(lng), row(lnb), wp2, row(bp2), cache0)
    return out, cache


def _ssd_kernel(x_ref, mod_ref, g_ref, win_ref, cw_ref, cb_ref, dtb_ref, alog_ref, dexp_ref, ng_ref,
                wout_ref, ehp_ref, ehs_ref, cache0_ref, st0_ref,
                o_ref, cache_ref, st_ref,
                z_ref, ext_ref, act_ref, y_ref, *, tm, valid, di, gn, n, heads):
    s = pl.program_id(1)
    taps = cw_ref.shape[0]
    cd = cw_ref.shape[2]
    pad = SUBLANES
    lc = SSD_CHUNK
    groups = gn // n
    hg = heads // groups
    p = di // heads
    gw = hg * p

    @pl.when(s == 0)
    def _():
        ext_ref[0:pad, :] = cache0_ref[...]
        st_ref[...] = st0_ref[...]

    x = x_ref[...]
    sh = mod_ref[3:4, :]
    sc = mod_ref[4:5, :]
    gt = mod_ref[5:6, :]
    hb = _modulated_rms(x, g_ref[...], sc, sh).astype(BF16)
    z_ref[...] = _dot(hb, win_ref[:, 0:di])
    dt_raw = _dot(hb, win_ref[:, di + cd:di + cd + LANES])

    first = pad - (taps - 1)
    nq = CONV_ROWS // SUBLANES
    slab = 4 * LANES
    for s0 in range(0, cd, slab):
        ext_ref[pad:pad + tm, s0:s0 + slab] = _dot(hb, win_ref[:, di + s0:di + s0 + slab])
        for c0 in range(s0, s0 + slab, LANES):
            for r0 in range(0, tm, CONV_ROWS):
                e = ext_ref[r0:r0 + CONV_ROWS + pad, c0:c0 + LANES]
                accs = [cb_ref[:, c0:c0 + LANES]] * nq
                for k in range(taps):
                    a, r = divmod(first + k, SUBLANES)
                    wk = cw_ref[k, :, c0:c0 + LANES]
                    ek = e if r == 0 else _shift_rows(e, r)
                    for q in range(nq):
                        accs[q] = accs[q] + wk * ek[SUBLANES * (a + q):SUBLANES * (a + q + 1), :]
                for q in range(nq):
                    act_ref[r0 + SUBLANES * q:r0 + SUBLANES * (q + 1), c0:c0 + LANES] = _silu(accs[q])
    cache_ref[...] = ext_ref[pad + valid - (taps - 1):pad + valid, :]

    ext_ref[0:pad, :] = ext_ref[tm:tm + pad, :]

    v = dt_raw + dtb_ref[...]
    dt = jnp.maximum(v, 0.0) + jnp.log1p(jnp.exp(-jnp.abs(v)))
    if valid < tm:
        dt = jnp.where(lax.broadcasted_iota(jnp.int32, dt.shape, 0) < valid, dt, 0.0)
    da = dt * (-jnp.exp(alog_ref[...]) * LOG2E)

    ri = lax.broadcasted_iota(jnp.int32, (tm, tm), 0)
    ci = lax.broadcasted_iota(jnp.int32, (tm, tm), 1)
    tril = jnp.where((ri >= ci) & (ri // lc == ci // lc), 1.0, 0.0).astype(BF16)
    d_hi = da.astype(BF16)
    d_r = da - d_hi.astype(F32)
    d_mid = d_r.astype(BF16)
    d_lo = (d_r - d_mid.astype(F32)).astype(BF16)
    cs = _dot(tril, jnp.concatenate([d_hi, d_mid, d_lo], axis=1))
    acum = _round16(cs[:, 0:LANES] + cs[:, LANES:2 * LANES] + cs[:, 2 * LANES:3 * LANES])
    acum_t = acum.T

    causal = (lax.broadcasted_iota(jnp.int32, (lc, lc), 0) >= lax.broadcasted_iota(jnp.int32, (lc, lc), 1))
    left = lax.broadcasted_iota(jnp.int32, (lc, 2 * p), 1) < p

    for c in range(tm // lc):
        r0 = c * lc
        ac = acum[r0:r0 + lc, :]
        al = ac[lc - 1:lc, :]
        stack = jnp.concatenate(
            [dt[r0:r0 + lc, :], jnp.exp2(ac), jnp.exp2(al - ac), jnp.broadcast_to(jnp.exp2(al), (SUBLANES, LANES))],
            axis=0)
        ex = _dot(_split2(stack), ehp_ref[...])
        dt_x = ex[0:lc, :]
        eac_x = ex[lc:2 * lc, :]
        wend_x = ex[2 * lc:3 * lc, :]
        dec_x = ex[3 * lc:3 * lc + 1, :]
        xs = act_ref[r0:r0 + lc, 0:di]
        xdt = xs * dt_x
        xw = (xdt * wend_x).astype(BF16)
        row_t = acum_t[:, r0:r0 + lc]
        ac2 = _split2(ac)
        y_parts = []
        for g in range(groups):
            bg = act_ref[r0:r0 + lc, di + g * n:di + (g + 1) * n]
            cgb = act_ref[r0:r0 + lc, di + gn + g * n:di + gn + (g + 1) * n].astype(BF16)
            scores = lax.dot_general(cgb, bg.astype(BF16), (((1,), (1,)), ((), ())),
                                     preferred_element_type=F32)
            h_g = st_ref[:, g * gw:(g + 1) * gw]
            y_int = _dot(cgb, h_g.astype(BF16))
            colb = _dot(ac2, ehs_ref[:, g * hg * lc:(g + 1) * hg * lc])
            ys = []
            for jp in range(hg // 2):
                wl = []
                for j in (2 * jp, 2 * jp + 1):
                    hh = g * hg + j
                    seg = colb[:, j * lc:(j + 1) * lc] - row_t[hh:hh + 1, :]
                    decay = jnp.exp2(jnp.where(causal, seg, -jnp.inf))
                    wl.append((scores * decay).astype(BF16))
                w2 = jnp.concatenate(wl, axis=1)
                xp = xdt[:, (g * hg + 2 * jp) * p:(g * hg + 2 * jp + 2) * p]
                x2 = jnp.concatenate([jnp.where(left, xp, 0.0), jnp.where(left, 0.0, xp)],
                                     axis=0).astype(BF16)
                ys.append(_dot(w2, x2))
            y_parts.append(jnp.concatenate(ys, axis=1) + y_int * eac_x[:, g * gw:(g + 1) * gw])
            st_ref[:, g * gw:(g + 1) * gw] = (h_g * dec_x[:, g * gw:(g + 1) * gw]
                                              + _dot(bg.T.astype(BF16), xw[:, g * gw:(g + 1) * gw]))
        y = jnp.concatenate(y_parts, axis=1) + dexp_ref[...] * xs
        y_ref[r0:r0 + lc, :] = y * _silu(z_ref[r0:r0 + lc, :])

    yv = y_ref[...]
    yn = yv * lax.rsqrt(jnp.mean(yv * yv, axis=-1, keepdims=True) + EPS) * ng_ref[...]
    o_ref[...] = x + gt * _dot(yn.astype(BF16), wout_ref[...])


def _ssd_call(x, mod, layer, boff, norm_g, gi, win, cw, cb, dtb, alog, dexp, ng, wout, ehp, ehs,
              cache0, st0, j, tm, valid, heads, n):
    b, s, d = x.shape
    taps, cd = cw.shape[-2:]
    di = wout.shape[-2]
    gn = (cd - di) // 2
    pw = win.shape[-1]
    assert s % tm == 0 and tm % SSD_CHUNK == 0 and tm % CONV_ROWS == 0 and (valid == tm or s == tm)
    row = lambda a: a.reshape(a.shape[0], 1, a.shape[-1])
    sub = lambda a: jnp.broadcast_to(a[..., None, :], a.shape[:-1] + (SUBLANES, a.shape[-1]))
    return pl.pallas_call(
        functools.partial(_ssd_kernel, tm=tm, valid=valid, di=di, gn=gn, n=n, heads=heads),
        grid=(b, s // tm),
        in_specs=[
            pl.BlockSpec((None, tm, d), lambda i, k: (i, k, 0)),
            pl.BlockSpec((None, None, N_MOD, d), lambda i, k: (layer, i + boff, 0, 0)),
            pl.BlockSpec((None, 1, d), lambda i, k: (gi, 0, 0)),
            _resident((None, d, pw), lambda i, k: (j, 0, 0)),
            pl.BlockSpec((None, taps, SUBLANES, cd), lambda i, k: (j, 0, 0, 0)),
            pl.BlockSpec((None, SUBLANES, cd), lambda i, k: (j, 0, 0)),
            pl.BlockSpec((None, 1, LANES), lambda i, k: (j, 0, 0)),
            pl.BlockSpec((None, 1, LANES), lambda i, k: (j, 0, 0)),
            pl.BlockSpec((None, 1, di), lambda i, k: (j, 0, 0)),
            pl.BlockSpec((None, 1, di), lambda i, k: (j, 0, 0)),
            _resident((None, di, d), lambda i, k: (j, 0, 0)),
            _resident(ehp.shape, lambda i, k: (0, 0)),
            _resident(ehs.shape, lambda i, k: (0, 0)),
            pl.BlockSpec((None, SUBLANES, cd), lambda i, k: (i, 0, 0)),
            pl.BlockSpec((None, n, di), lambda i, k: (i, 0, 0)),
        ],
        out_specs=[
            pl.BlockSpec((None, tm, d), lambda i, k: (i, k, 0)),
            pl.BlockSpec((None, taps - 1, cd), lambda i, k: (i, 0, 0)),
            pl.BlockSpec((None, n, di), lambda i, k: (i, 0, 0)),
        ],
        out_shape=[
            jax.ShapeDtypeStruct((b, s, d), F32),
            jax.ShapeDtypeStruct((b, taps - 1, cd), F32),
            jax.ShapeDtypeStruct((b, n, di), F32),
        ],
        scratch_shapes=[
            pltpu.VMEM((tm, di), F32),
            pltpu.VMEM((tm + SUBLANES, cd), F32),
            pltpu.VMEM((tm, cd), F32),
            pltpu.VMEM((tm, di), F32),
        ],
        compiler_params=pltpu.CompilerParams(
            dimension_semantics=("arbitrary", "arbitrary"), vmem_limit_bytes=VMEM_LIMIT),
        name="ssd_mixer",
    )(x, mod, norm_g, win, sub(cw), sub(cb), row(dtb), row(alog), row(dexp), row(ng), wout, ehp, ehs, cache0, st0)


def _head_expanders(heads, p, lc):
    h_of_row = jnp.arange(2 * LANES) % LANES
    ehp = (h_of_row[:, None] == (jnp.arange(heads * p) // p)[None, :]).astype(BF16)
    ehs = (h_of_row[:, None] == (jnp.arange(heads * lc) // lc)[None, :]).astype(BF16)
    return ehp, ehs


def _trunk(x, mod, boff, valid, tiles, ssd_state_t, ssd_conv_cache, cmod_conv_cache, wts):
    (norm_g, w1, w3, w2, win, cw, cb, dtb, alog, dexp, ssd_ng, wout, ehp, ehs,
     wp1, bp1, dww, dwb, lng, lnb, wp2, bp2, final_g, heads, n) = wts
    depth = norm_g.shape[0] // 3
    tm_ffn, tm_ssd, tm_conv = tiles
    new_state, new_ssd_conv, new_cmod_conv = [], [], []
    for i in range(depth):
        x = _ffn_call(x, mod, i, boff, 0, norm_g, 3 * i, w1, w3, w2, i, 0, tm_ffn)
        j = i // 2
        if i % 2 == 0:
            x, cache, st = _ssd_call(x, mod, i, boff, norm_g, 3 * i + 1, win, cw, cb, dtb, alog, dexp, ssd_ng,
                                     wout, ehp, ehs, ssd_conv_cache[j], ssd_state_t[j], j, tm_ssd,
                                     min(valid, tm_ssd), heads, n)
            new_ssd_conv.append(cache)
            new_state.append(st)
        else:
            x, cache = _cmod_call(x, mod, i, boff, norm_g, 3 * i + 1, wp1, bp1, dww, dwb, lng, lnb, wp2, bp2,
                                  cmod_conv_cache[j], j, tm_conv, min(valid, tm_conv))
            new_cmod_conv.append(cache)
        x = _ffn_call(x, mod, i, boff, 6, norm_g, 3 * i + 2, w1, w3, w2, i, 1, tm_ffn,
                      final_g=final_g if i == depth - 1 else None)
    return x, jnp.stack(new_state), jnp.stack(new_ssd_conv), jnp.stack(new_cmod_conv)


def kernel(x_prompt, x_sample, c_prompt, c_sample, state_ssd, cache_ssd_conv, cache_cmod_conv, mod_w, mod_b, norm_g, ffn_w1, ffn_w3, ffn_w2, ssd_w_in, ssd_conv_w, ssd_conv_b, ssd_dt_bias, ssd_a_log, ssd_d, ssd_norm_g, ssd_w_out, cmod_w_pw1, cmod_b_pw1, cmod_dw_w, cmod_dw_b, cmod_ln_g, cmod_ln_b, cmod_w_pw2, cmod_b_pw2, final_g):
    bp, sp, d = x_prompt.shape
    bs, ss, _ = x_sample.shape
    n_ssd, _, heads, p, n = state_ssd.shape
    di = heads * p
    cd = ssd_conv_w.shape[-1]
    depth = mod_w.shape[0]
    assert heads <= LANES and ss <= SSD_CHUNK

    mod = _mod_call(jnp.concatenate([c_prompt, c_sample], axis=0), mod_w, mod_b)
    mod = mod.reshape(depth, bp + bs, N_MOD, d)

    lane_pad = lambda a: jnp.pad(a, ((0, 0), (0, LANES - a.shape[-1])))
    win = jnp.concatenate(
        [ssd_w_in[..., :di + cd], jnp.pad(ssd_w_in[..., di + cd:], ((0, 0), (0, 0), (0, LANES - heads)))],
        axis=-1).astype(BF16)
    ehp, ehs = _head_expanders(heads, p, SSD_CHUNK)
    wts = (norm_g.reshape(depth * 3, 1, d), ffn_w1.astype(BF16), ffn_w3.astype(BF16), ffn_w2.astype(BF16),
           win, ssd_conv_w, ssd_conv_b, lane_pad(ssd_dt_bias), lane_pad(ssd_a_log),
           jnp.repeat(ssd_d, p, axis=-1), ssd_norm_g, ssd_w_out.astype(BF16), ehp, ehs,
           cmod_w_pw1.astype(BF16), cmod_b_pw1, cmod_dw_w, cmod_dw_b, cmod_ln_g, cmod_ln_b,
           cmod_w_pw2.astype(BF16), cmod_b_pw2, final_g.reshape(1, d), heads, n)

    def to_t(st):
        return jnp.swapaxes(st.reshape(st.shape[0], st.shape[1], di, n), 2, 3)

    def from_t(st_t):
        return jnp.swapaxes(st_t, 2, 3).reshape(st_t.shape[0], st_t.shape[1], heads, p, n)

    cmod_pad = 32 - cache_cmod_conv.shape[2]
    ssd_pad = SUBLANES - cache_ssd_conv.shape[2]

    zeros = lambda *shape: jnp.zeros(shape, F32)
    y_p, st_p, sc_p, cc_p = _trunk(
        x_prompt, mod, 0, sp, (512, 256, 256),
        zeros(n_ssd, bp, n, di), zeros(n_ssd, bp, SUBLANES, cd),
        zeros(cache_cmod_conv.shape[0], bp, 32, cache_cmod_conv.shape[-1]), wts)

    xs_pad = jnp.pad(x_sample, ((0, 0), (0, SSD_CHUNK - ss), (0, 0)))
    y_s, st_s, sc_s, cc_s = _trunk(
        xs_pad, mod, bp, ss, (SSD_CHUNK, SSD_CHUNK, SSD_CHUNK),
        to_t(state_ssd), jnp.pad(cache_ssd_conv, ((0, 0), (0, 0), (ssd_pad, 0), (0, 0))),
        jnp.pad(cache_cmod_conv, ((0, 0), (0, 0), (cmod_pad, 0), (0, 0))), wts)

    return (y_p, y_s[:, :ss], from_t(st_p), sc_p, cc_p, from_t(st_s), sc_s, cc_s)
```

```python
import functools

import jax
import jax.numpy as jnp
from jax import lax
from jax.experimental import pallas as pl
from jax.experimental.pallas import tpu as pltpu

F32 = jnp.float32
BF16 = jnp.bfloat16
EPS = 1e-6
LOG2E = 1.4426950408889634
N_MOD = 9
LANES = 128
SUBLANES = 8
PACKED_ROWS = 16
SSD_CHUNK = 128
VMEM_LIMIT = 56 * 1024 * 1024


def _dot(a, b):
    return jnp.dot(a, b, preferred_element_type=F32)


def _silu(v):
    return v * jax.nn.sigmoid(v)


def _modulated_rms(x, g, sc, sh):
    y = x * lax.rsqrt(jnp.mean(x * x, axis=-1, keepdims=True) + EPS) * g
    return y * (1.0 + sc) + sh


def _split2(v):
    hi = v.astype(BF16)
    lo = (v - hi.astype(F32)).astype(BF16)
    return jnp.concatenate([hi, lo], axis=1)


def _round16(v):
    hi = v.astype(BF16)
    lo = (v - hi.astype(F32)).astype(BF16)
    return hi.astype(F32) + lo.astype(F32)


def _resident(shape, index_map):
    return pl.BlockSpec(shape, index_map, pipeline_mode=pl.Buffered(1))


def _mod_kernel(c_ref, w_ref, b_ref, o_ref):
    c = c_ref[...]
    o_ref[...] = _dot(_silu(c).astype(BF16), w_ref[...].astype(BF16)) + b_ref[...]


def _mod_call(c_all, mod_w, mod_b):
    depth, d, nm = mod_w.shape
    rows = c_all.shape[0]
    tn = 1152
    assert nm % tn == 0
    return pl.pallas_call(
        _mod_kernel,
        grid=(depth, nm // tn),
        in_specs=[
            pl.BlockSpec((rows, d), lambda i, j: (0, 0)),
            pl.BlockSpec((None, d, tn), lambda i, j: (i, 0, j)),
            pl.BlockSpec((None, 1, tn), lambda i, j: (i, 0, j)),
        ],
        out_specs=pl.BlockSpec((None, rows, tn), lambda i, j: (i, 0, j)),
        out_shape=jax.ShapeDtypeStruct((depth, rows, nm), F32),
        compiler_params=pltpu.CompilerParams(
            dimension_semantics=("arbitrary", "arbitrary"), vmem_limit_bytes=VMEM_LIMIT),
        name="adaln_mod",
    )(c_all, mod_w, mod_b.reshape(depth, 1, nm))


def _ffn_kernel(x_ref, mod_ref, g_ref, w1_ref, w3_ref, w2_ref, *rest, mod_row, n_chunks, final):
    if final:
        fg_ref, o_ref, hid_ref = rest
    else:
        o_ref, hid_ref = rest
    x = x_ref[...]
    sh = mod_ref[mod_row:mod_row + 1, :]
    sc = mod_ref[mod_row + 1:mod_row + 2, :]
    gt = mod_ref[mod_row + 2:mod_row + 3, :]
    hb = _modulated_rms(x, g_ref[...], sc, sh).astype(BF16)
    fc = w1_ref.shape[1] // n_chunks
    for c in range(n_chunks):
        a = _dot(hb, w1_ref[:, c * fc:(c + 1) * fc])
        b = _dot(hb, w3_ref[:, c * fc:(c + 1) * fc])
        hid_ref[:, c * fc:(c + 1) * fc] = (_silu(a) * b).astype(BF16)
    y = x + (0.5 * gt) * _dot(hid_ref[...], w2_ref[...])
    if final:
        y = y * lax.rsqrt(jnp.mean(y * y, axis=-1, keepdims=True) + EPS) * fg_ref[...]
    o_ref[...] = y


def _ffn_call(x, mod, layer, boff, mod_row, norm_g, gi, w1, w3, w2, li, lj, tm, final_g=None):
    b, s, d = x.shape
    dff = w1.shape[-1]
    n_chunks = dff // (2 * LANES)
    assert s % tm == 0 and dff % (2 * LANES) == 0
    in_specs = [
        pl.BlockSpec((None, tm, d), lambda i, j: (i, j, 0)),
        pl.BlockSpec((None, None, N_MOD, d), lambda i, j: (layer, i + boff, 0, 0)),
        pl.BlockSpec((None, 1, d), lambda i, j: (gi, 0, 0)),
        _resident((None, None, d, dff), lambda i, j: (li, lj, 0, 0)),
        _resident((None, None, d, dff), lambda i, j: (li, lj, 0, 0)),
        _resident((None, None, dff, d), lambda i, j: (li, lj, 0, 0)),
    ]
    args = [x, mod, norm_g, w1, w3, w2]
    if final_g is not None:
        in_specs.append(pl.BlockSpec((1, d), lambda i, j: (0, 0)))
        args.append(final_g)
    return pl.pallas_call(
        functools.partial(_ffn_kernel, mod_row=mod_row, n_chunks=n_chunks, final=final_g is not None),
        grid=(b, s // tm),
        in_specs=in_specs,
        out_specs=pl.BlockSpec((None, tm, d), lambda i, j: (i, j, 0)),
        out_shape=jax.ShapeDtypeStruct((b, s, d), F32),
        scratch_shapes=[pltpu.VMEM((tm, dff), BF16)],
        compiler_params=pltpu.CompilerParams(
            dimension_semantics=("arbitrary", "arbitrary"), vmem_limit_bytes=VMEM_LIMIT),
        name="swiglu_ffn",
    )(*args)


CONV_ROWS = 64


def _shift_rows(e, r):
    return pltpu.roll(e, e.shape[0] - r, axis=0)


def _aligned(i):
    return pl.multiple_of(i, SUBLANES)


def _cmod_kernel(x_ref, mod_ref, g_ref, wp1_ref, bp1_ref, dww_ref, dwb_ref, lng_ref, lnb_ref,
                 wp2_ref, bp2_ref, cache0_ref, o_ref, cache_ref, hb_ref, ext_ref, sh_ref, v_ref, *, tm, valid, pad):
    s = pl.program_id(1)
    taps = dww_ref.shape[0]
    dc = dww_ref.shape[1] * LANES
    pk = PACKED_ROWS

    @pl.when(s == 0)
    def _():
        ext_ref[0:pad, :] = cache0_ref[...]

    x = x_ref[...]
    sh = mod_ref[3:4, :]
    sc = mod_ref[4:5, :]
    gt = mod_ref[5:6, :]
    hb_ref[...] = _modulated_rms(x, g_ref[...], sc, sh).astype(BF16)
    keep = sh_ref.shape[2] - pk
    first = pad - (taps - 1)
    nq = CONV_ROWS // pk
    slab = 2 * LANES

    def glu_slab(s0):
        hb = hb_ref[...]
        ua = _dot(hb, wp1_ref[:, s0:s0 + slab]) + bp1_ref[:, s0:s0 + slab]
        ug = _dot(hb, wp1_ref[:, dc + s0:dc + s0 + slab]) + bp1_ref[:, dc + s0:dc + s0 + slab]
        ext_ref[pad:pad + tm, s0:s0 + slab] = ua * jax.nn.sigmoid(ug)

    def conv_slab(s0):
        for c0 in range(s0, s0 + slab, LANES):
            ct = c0 // LANES
            e = ext_ref[:, c0:c0 + LANES]
            for r in range(SUBLANES):
                er = e if r == 0 else _shift_rows(e, r)
                sh_ref[r, ct, :, :] = er.astype(BF16)
                sh_ref[r + SUBLANES, ct, 0:keep, :] = er[SUBLANES:SUBLANES + keep, :].astype(BF16)
            for r0 in range(0, tm, CONV_ROWS):
                accs = [None] * nq
                for k in range(taps):
                    a, r = divmod(first + k, pk)
                    wk = dww_ref[k, ct].astype(F32)
                    for q in range(nq):
                        lo = r0 + pk * (a + q)
                        term = wk * sh_ref[r, ct, lo:lo + pk, :].astype(F32)
                        accs[q] = term if accs[q] is None else accs[q] + term
                for q in range(nq):
                    v_ref[r0 + pk * q:r0 + pk * (q + 1), c0:c0 + LANES] = accs[q] + dwb_ref[:, c0:c0 + LANES]

    for s0 in range(0, dc, slab):
        glu_slab(s0)
    for s0 in range(0, dc, slab):
        @pl.when(s >= 0)
        def _():
            conv_slab(s0)

    cache_ref[...] = ext_ref[pad + valid - (taps - 1):pad + valid, :]
    ext_ref[0:pad, :] = ext_ref[tm:tm + pad, :]

    v = v_ref[...]
    vc = v - jnp.mean(v, axis=-1, keepdims=True)
    y = vc * lax.rsqrt(jnp.mean(vc * vc, axis=-1, keepdims=True) + EPS) * lng_ref[...] + lnb_ref[...]
    out = _dot(_silu(y).astype(BF16), wp2_ref[...]) + bp2_ref[...]
    o_ref[...] = x_ref[...] + gt * out


def _cmod_call(x, mod, layer, boff, norm_g, gi, wp1, bp1, dww, dwb, lng, lnb, wp2, bp2, cache0, j, tm, valid):
    b, s, d = x.shape
    taps, dc = dww.shape[-2:]
    pad = 32
    assert s % tm == 0 and tm % CONV_ROWS == 0 and taps - 1 <= pad and (valid == tm or s == tm)
    assert dc % LANES == 0
    row = lambda a: a.reshape(a.shape[0], 1, a.shape[-1])
    sub = lambda a: jnp.broadcast_to(a[..., None, :], a.shape[:-1] + (PACKED_ROWS, a.shape[-1]))
    dww_t = jnp.broadcast_to(dww.astype(BF16).reshape(dww.shape[0], taps, dc // LANES, 1, LANES),
                             (dww.shape[0], taps, dc // LANES, PACKED_ROWS, LANES))
    out, cache = pl.pallas_call(
        functools.partial(_cmod_kernel, tm=tm, valid=valid, pad=pad),
        grid=(b, s // tm),
        in_specs=[
            pl.BlockSpec((None, tm, d), lambda i, k: (i, k, 0)),
            pl.BlockSpec((None, None, N_MOD, d), lambda i, k: (layer, i + boff, 0, 0)),
            pl.BlockSpec((None, 1, d), lambda i, k: (gi, 0, 0)),
            _resident((None, d, 2 * dc), lambda i, k: (j, 0, 0)),
            pl.BlockSpec((None, 1, 2 * dc), lambda i, k: (j, 0, 0)),
            pl.BlockSpec((None, taps, dc // LANES, PACKED_ROWS, LANES), lambda i, k: (j, 0, 0, 0, 0)),
            pl.BlockSpec((None, PACKED_ROWS, dc), lambda i, k: (j, 0, 0)),
            pl.BlockSpec((None, 1, dc), lambda i, k: (j, 0, 0)),
            pl.BlockSpec((None, 1, dc), lambda i, k: (j, 0, 0)),
            _resident((None, dc, d), lambda i, k: (j, 0, 0)),
            pl.BlockSpec((None, 1, d), lambda i, k: (j, 0, 0)),
            pl.BlockSpec((None, pad, dc), lambda i, k: (i, 0, 0)),
        ],
        out_specs=[
            pl.BlockSpec((None, tm, d), lambda i, k: (i, k, 0)),
            pl.BlockSpec((None, taps - 1, dc), lambda i, k: (i, 0, 0)),
        ],
        out_shape=[
            jax.ShapeDtypeStruct((b, s, d), F32),
            jax.ShapeDtypeStruct((b, taps - 1, dc), F32),
        ],
        scratch_shapes=[
            pltpu.VMEM((tm, d), BF16),
            pltpu.VMEM((tm + pad, dc), F32),
            pltpu.VMEM((PACKED_ROWS, dc // LANES, tm + pad, LANES), BF16),
            pltpu.VMEM((tm, dc), F32),
        ],
        compiler_params=pltpu.CompilerParams(
            dimension_semantics=("arbitrary", "arbitrary"), vmem_limit_bytes=VMEM_LIMIT),
        name="conformer_conv",
    )(x, mod, norm_g, wp1, row(bp1), dww_t, sub(dwb), row(lng), row(lnb), wp2, row(bp2), cache0)
    return out, cache


def _ssd_kernel(x_ref, mod_ref, g_ref, win_ref, cw_ref, cb_ref, dtb_ref, alog_ref, dexp_ref, ng_ref,
                wout_ref, ehp_ref, ehs_ref, cache0_ref, st0_ref,
                o_ref, cache_ref, st_ref,
                hb_ref, ext_ref, act_ref, dtr_ref, z_ref, *, tm, valid, di, gn, n, heads):
    s = pl.program_id(1)
    taps = cw_ref.shape[0]
    cd = cw_ref.shape[2]
    pad = SUBLANES
    lc = SSD_CHUNK
    groups = gn // n
    hg = heads // groups
    p = di // heads
    gw = hg * p

    @pl.when(s == 0)
    def _():
        ext_ref[0:pad, :] = cache0_ref[...]
        st_ref[...] = st0_ref[...]

    sh = mod_ref[3:4, :]
    sc = mod_ref[4:5, :]
    gt = mod_ref[5:6, :]
    hb_ref[...] = _modulated_rms(x_ref[...], g_ref[...], sc, sh).astype(BF16)

    first = pad - (taps - 1)
    nq = CONV_ROWS // SUBLANES
    slab = 4 * LANES
    neg_a = -jnp.exp(alog_ref[...]) * LOG2E
    causal = (lax.broadcasted_iota(jnp.int32, (lc, lc), 0) >= lax.broadcasted_iota(jnp.int32, (lc, lc), 1))
    tril = jnp.where(causal, 1.0, 0.0).astype(BF16)
    left = lax.broadcasted_iota(jnp.int32, (lc, 2 * p), 1) < p

    def xbc_dt_pieces(c):
        lo = c * lc

        def xbc(s0):
            ext_ref[pad + lo:pad + lo + lc, s0:s0 + slab] = _dot(hb_ref[lo:lo + lc, :],
                                                                 win_ref[:, di + s0:di + s0 + slab])

        def dtp():
            dtr_ref[lo:lo + lc, :] = _dot(hb_ref[lo:lo + lc, :], win_ref[:, di + cd:di + cd + LANES])

        return [functools.partial(xbc, s0) for s0 in range(0, cd, slab)] + [dtp]

    def gate_pieces(c):
        lo = c * lc

        def zp(s0):
            z_ref[:, s0:s0 + slab] = _dot(hb_ref[lo:lo + lc, :], win_ref[:, s0:s0 + slab])

        return [functools.partial(zp, s0) for s0 in range(0, di, slab)]

    n_chunks = tm // lc
    for piece in xbc_dt_pieces(0):
        piece()
    for c in range(n_chunks):
        r0 = c * lc
        pending = gate_pieces(c) + (xbc_dt_pieces(c + 1) if c + 1 < n_chunks else [])
        tiles_per_piece = max(1, (cd // LANES) // len(pending))
        for ti, c0 in enumerate(range(0, cd, LANES)):
            for rb in range(r0, r0 + lc, CONV_ROWS):
                e = ext_ref[rb:rb + CONV_ROWS + pad, c0:c0 + LANES]
                accs = [cb_ref[:, c0:c0 + LANES]] * nq
                for k in range(taps):
                    a, r = divmod(first + k, SUBLANES)
                    wk = cw_ref[k, :, c0:c0 + LANES]
                    ek = e if r == 0 else _shift_rows(e, r)
                    for q in range(nq):
                        accs[q] = accs[q] + wk * ek[SUBLANES * (a + q):SUBLANES * (a + q + 1), :]
                for q in range(nq):
                    act_ref[rb + SUBLANES * q:rb + SUBLANES * (q + 1), c0:c0 + LANES] = _silu(accs[q])
            if ti % tiles_per_piece == tiles_per_piece - 1 and pending:
                pending.pop(0)()
        for piece in pending:
            piece()

        v = dtr_ref[r0:r0 + lc, :] + dtb_ref[...]
        dt = jnp.maximum(v, 0.0) + jnp.log1p(jnp.exp(-jnp.abs(v)))
        if valid < r0 + lc:
            dt = jnp.where(lax.broadcasted_iota(jnp.int32, dt.shape, 0) < valid - r0, dt, 0.0)
        da = dt * neg_a
        d_hi = da.astype(BF16)
        d_r = da - d_hi.astype(F32)
        d_mid = d_r.astype(BF16)
        d_lo = (d_r - d_mid.astype(F32)).astype(BF16)
        cs = _dot(tril, jnp.concatenate([d_hi, d_mid, d_lo], axis=1))
        ac = _round16(cs[:, 0:LANES] + cs[:, LANES:2 * LANES] + cs[:, 2 * LANES:3 * LANES])
        row_t = ac.T - jnp.log2(dt.T)

        al = ac[lc - 1:lc, :]
        stack = jnp.concatenate(
            [jnp.exp2(ac), jnp.exp2(al - ac) * dt, jnp.broadcast_to(jnp.exp2(al), (SUBLANES, LANES))], axis=0)
        ex = _dot(_split2(stack), ehp_ref[...])
        eac_x = ex[0:lc, :]
        wend_x = ex[lc:2 * lc, :]
        dec_x = ex[2 * lc:2 * lc + 1, :]
        xs = act_ref[r0:r0 + lc, 0:di]
        xw = (xs * wend_x).astype(BF16)
        ac2 = _split2(ac)
        y_parts = []
        for g in range(groups):
            bg = act_ref[r0:r0 + lc, di + g * n:di + (g + 1) * n]
            cgb = act_ref[r0:r0 + lc, di + gn + g * n:di + gn + (g + 1) * n].astype(BF16)
            scores = lax.dot_general(cgb, bg.astype(BF16), (((1,), (1,)), ((), ())),
                                     preferred_element_type=F32)
            h_g = st_ref[:, g * gw:(g + 1) * gw]
            y_int = _dot(cgb, h_g.astype(BF16))
            colb = _dot(ac2, ehs_ref[:, g * hg * lc:(g + 1) * hg * lc])
            ys = []
            for jp in range(hg // 2):
                wl = []
                for j in (2 * jp, 2 * jp + 1):
                    hh = g * hg + j
                    seg = colb[:, j * lc:(j + 1) * lc] - row_t[hh:hh + 1, :]
                    decay = jnp.exp2(jnp.where(causal, seg, -jnp.inf))
                    wl.append((scores * decay).astype(BF16))
                w2 = jnp.concatenate(wl, axis=1)
                xp = xs[:, (g * hg + 2 * jp) * p:(g * hg + 2 * jp + 2) * p]
                x2 = jnp.concatenate([jnp.where(left, xp, 0.0), jnp.where(left, 0.0, xp)],
                                     axis=0).astype(BF16)
                ys.append(_dot(w2, x2))
            y_parts.append(jnp.concatenate(ys, axis=1) + y_int * eac_x[:, g * gw:(g + 1) * gw])
            st_ref[:, g * gw:(g + 1) * gw] = (h_g * dec_x[:, g * gw:(g + 1) * gw]
                                              + _dot(bg.T.astype(BF16), xw[:, g * gw:(g + 1) * gw]))
        y = jnp.concatenate(y_parts, axis=1) + dexp_ref[...] * xs
        yv = y * _silu(z_ref[...])
        yn = yv * lax.rsqrt(jnp.mean(yv * yv, axis=-1, keepdims=True) + EPS) * ng_ref[...]
        o_ref[r0:r0 + lc, :] = x_ref[r0:r0 + lc, :] + gt * _dot(yn.astype(BF16), wout_ref[...])

    cache_ref[...] = ext_ref[pad + valid - (taps - 1):pad + valid, :]
    ext_ref[0:pad, :] = ext_ref[tm:tm + pad, :]


def _ssd_call(x, mod, layer, boff, norm_g, gi, win, cw, cb, dtb, alog, dexp, ng, wout, ehp, ehs,
              cache0, st0, j, tm, valid, heads, n):
    b, s, d = x.shape
    taps, cd = cw.shape[-2:]
    di = wout.shape[-2]
    gn = (cd - di) // 2
    pw = win.shape[-1]
    assert s % tm == 0 and tm % SSD_CHUNK == 0 and tm % CONV_ROWS == 0 and (valid == tm or s == tm)
    row = lambda a: a.reshape(a.shape[0], 1, a.shape[-1])
    sub = lambda a: jnp.broadcast_to(a[..., None, :], a.shape[:-1] + (SUBLANES, a.shape[-1]))
    return pl.pallas_call(
        functools.partial(_ssd_kernel, tm=tm, valid=valid, di=di, gn=gn, n=n, heads=heads),
        grid=(b, s // tm),
        in_specs=[
            pl.BlockSpec((None, tm, d), lambda i, k: (i, k, 0)),
            pl.BlockSpec((None, None, N_MOD, d), lambda i, k: (layer, i + boff, 0, 0)),
            pl.BlockSpec((None, 1, d), lambda i, k: (gi, 0, 0)),
            _resident((None, d, pw), lambda i, k: (j, 0, 0)),
            pl.BlockSpec((None, taps, SUBLANES, cd), lambda i, k: (j, 0, 0, 0)),
            pl.BlockSpec((None, SUBLANES, cd), lambda i, k: (j, 0, 0)),
            pl.BlockSpec((None, 1, LANES), lambda i, k: (j, 0, 0)),
            pl.BlockSpec((None, 1, LANES), lambda i, k: (j, 0, 0)),
            pl.BlockSpec((None, 1, di), lambda i, k: (j, 0, 0)),
            pl.BlockSpec((None, 1, di), lambda i, k: (j, 0, 0)),
            _resident((None, di, d), lambda i, k: (j, 0, 0)),
            _resident(ehp.shape, lambda i, k: (0, 0)),
            _resident(ehs.shape, lambda i, k: (0, 0)),
            pl.BlockSpec((None, SUBLANES, cd), lambda i, k: (i, 0, 0)),
            pl.BlockSpec((None, n, di), lambda i, k: (i, 0, 0)),
        ],
        out_specs=[
            pl.BlockSpec((None, tm, d), lambda i, k: (i, k, 0)),
            pl.BlockSpec((None, taps - 1, cd), lambda i, k: (i, 0, 0)),
            pl.BlockSpec((None, n, di), lambda i, k: (i, 0, 0)),
        ],
        out_shape=[
            jax.ShapeDtypeStruct((b, s, d), F32),
            jax.ShapeDtypeStruct((b, taps - 1, cd), F32),
            jax.ShapeDtypeStruct((b, n, di), F32),
        ],
        scratch_shapes=[
            pltpu.VMEM((tm, d), BF16),
            pltpu.VMEM((tm + SUBLANES, cd), F32),
            pltpu.VMEM((tm, cd), F32),
            pltpu.VMEM((tm, LANES), F32),
            pltpu.VMEM((SSD_CHUNK, di), F32),
        ],
        compiler_params=pltpu.CompilerParams(
            dimension_semantics=("arbitrary", "arbitrary"), vmem_limit_bytes=VMEM_LIMIT),
        name="ssd_mixer",
    )(x, mod, norm_g, win, sub(cw), sub(cb), row(dtb), row(alog), row(dexp), row(ng), wout, ehp, ehs, cache0, st0)


def _head_expanders(heads, p, lc):
    h_of_row = jnp.arange(2 * LANES) % LANES
    ehp = (h_of_row[:, None] == (jnp.arange(heads * p) // p)[None, :]).astype(BF16)
    ehs = (h_of_row[:, None] == (jnp.arange(heads * lc) // lc)[None, :]).astype(BF16)
    return ehp, ehs


def _trunk(x, mod, boff, valid, tiles, ssd_state_t, ssd_conv_cache, cmod_conv_cache, wts):
    (norm_g, w1, w3, w2, win, cw, cb, dtb, alog, dexp, ssd_ng, wout, ehp, ehs,
     wp1, bp1, dww, dwb, lng, lnb, wp2, bp2, final_g, heads, n) = wts
    depth = norm_g.shape[0] // 3
    tm_ffn, tm_ssd, tm_conv = tiles
    new_state, new_ssd_conv, new_cmod_conv = [], [], []
    for i in range(depth):
        x = _ffn_call(x, mod, i, boff, 0, norm_g, 3 * i, w1, w3, w2, i, 0, tm_ffn)
        j = i // 2
        if i % 2 == 0:
            x, cache, st = _ssd_call(x, mod, i, boff, norm_g, 3 * i + 1, win, cw, cb, dtb, alog, dexp, ssd_ng,
                                     wout, ehp, ehs, ssd_conv_cache[j], ssd_state_t[j], j, tm_ssd,
                                     min(valid, tm_ssd), heads, n)
            new_ssd_conv.append(cache)
            new_state.append(st)
        else:
            x, cache = _cmod_call(x, mod, i, boff, norm_g, 3 * i + 1, wp1, bp1, dww, dwb, lng, lnb, wp2, bp2,
                                  cmod_conv_cache[j], j, tm_conv, min(valid, tm_conv))
            new_cmod_conv.append(cache)
        x = _ffn_call(x, mod, i, boff, 6, norm_g, 3 * i + 2, w1, w3, w2, i, 1, tm_ffn,
                      final_g=final_g if i == depth - 1 else None)
    return x, jnp.stack(new_state), jnp.stack(new_ssd_conv), jnp.stack(new_cmod_conv)


def kernel(x_prompt, x_sample, c_prompt, c_sample, state_ssd, cache_ssd_conv, cache_cmod_conv, mod_w, mod_b, norm_g, ffn_w1, ffn_w3, ffn_w2, ssd_w_in, ssd_conv_w, ssd_conv_b, ssd_dt_bias, ssd_a_log, ssd_d, ssd_norm_g, ssd_w_out, cmod_w_pw1, cmod_b_pw1, cmod_dw_w, cmod_dw_b, cmod_ln_g, cmod_ln_b, cmod_w_pw2, cmod_b_pw2, final_g):
    bp, sp, d = x_prompt.shape
    bs, ss, _ = x_sample.shape
    n_ssd, _, heads, p, n = state_ssd.shape
    di = heads * p
    cd = ssd_conv_w.shape[-1]
    depth = mod_w.shape[0]
    assert heads <= LANES and ss <= SSD_CHUNK

    mod = _mod_call(jnp.concatenate([c_prompt, c_sample], axis=0), mod_w, mod_b)
    mod = mod.reshape(depth, bp + bs, N_MOD, d)

    lane_pad = lambda a: jnp.pad(a, ((0, 0), (0, LANES - a.shape[-1])))
    win = jnp.concatenate(
        [ssd_w_in[..., :di + cd], jnp.pad(ssd_w_in[..., di + cd:], ((0, 0), (0, 0), (0, LANES - heads)))],
        axis=-1).astype(BF16)
    ehp, ehs = _head_expanders(heads, p, SSD_CHUNK)
    wts = (norm_g.reshape(depth * 3, 1, d), ffn_w1.astype(BF16), ffn_w3.astype(BF16), ffn_w2.astype(BF16),
           win, ssd_conv_w, ssd_conv_b, lane_pad(ssd_dt_bias), lane_pad(ssd_a_log),
           jnp.repeat(ssd_d, p, axis=-1), ssd_norm_g, ssd_w_out.astype(BF16), ehp, ehs,
           cmod_w_pw1.astype(BF16), cmod_b_pw1, cmod_dw_w, cmod_dw_b, cmod_ln_g, cmod_ln_b,
           cmod_w_pw2.astype(BF16), cmod_b_pw2, final_g.reshape(1, d), heads, n)

    def to_t(st):
        return jnp.swapaxes(st.reshape(st.shape[0], st.shape[1], di, n), 2, 3)

    def from_t(st_t):
        return jnp.swapaxes(st_t, 2, 3).reshape(st_t.shape[0], st_t.shape[1], heads, p, n)

    cmod_pad = 32 - cache_cmod_conv.shape[2]
    ssd_pad = SUBLANES - cache_ssd_conv.shape[2]

    zeros = lambda *shape: jnp.zeros(shape, F32)
    y_p, st_p, sc_p, cc_p = _trunk(
        x_prompt, mod, 0, sp, (512, 256, 256),
        zeros(n_ssd, bp, n, di), zeros(n_ssd, bp, SUBLANES, cd),
        zeros(cache_cmod_conv.shape[0], bp, 32, cache_cmod_conv.shape[-1]), wts)

    xs_pad = jnp.pad(x_sample, ((0, 0), (0, SSD_CHUNK - ss), (0, 0)))
    y_s, st_s, sc_s, cc_s = _trunk(
        xs_pad, mod, bp, ss, (SSD_CHUNK, SSD_CHUNK, SSD_CHUNK),
        to_t(state_ssd), jnp.pad(cache_ssd_conv, ((0, 0), (0, 0), (ssd_pad, 0), (0, 0))),
        jnp.pad(cache_cmod_conv, ((0, 0), (0, 0), (cmod_pad, 0), (0, 0))), wts)

    return (y_p, y_s[:, :ss], from_t(st_p), sc_p, cc_p, from_t(st_s), sc_s, cc_s)
```

```python
import functools

import jax
import jax.numpy as jnp
from jax import lax
from jax.experimental import pallas as pl
from jax.experimental.pallas import tpu as pltpu

F32 = jnp.float32
BF16 = jnp.bfloat16
EPS = 1e-6
LOG2E = 1.4426950408889634
N_MOD = 9
LANES = 128
SUBLANES = 8
PACKED_ROWS = 16
SSD_CHUNK = 128
VMEM_LIMIT = 56 * 1024 * 1024


def _dot(a, b):
    return jnp.dot(a, b, preferred_element_type=F32)


def _silu(v):
    return v * jax.nn.sigmoid(v)


def _modulated_rms(x, g, sc, sh):
    y = x * lax.rsqrt(jnp.mean(x * x, axis=-1, keepdims=True) + EPS) * g
    return y * (1.0 + sc) + sh


def _split2(v):
    hi = v.astype(BF16)
    lo = (v - hi.astype(F32)).astype(BF16)
    return jnp.concatenate([hi, lo], axis=1)


def _round16(v):
    hi = v.astype(BF16)
    lo = (v - hi.astype(F32)).astype(BF16)
    return hi.astype(F32) + lo.astype(F32)


def _resident(shape, index_map):
    return pl.BlockSpec(shape, index_map, pipeline_mode=pl.Buffered(1))


def _mod_kernel(c_ref, w_ref, b_ref, o_ref):
    c = c_ref[...]
    o_ref[...] = _dot(_silu(c).astype(BF16), w_ref[...].astype(BF16)) + b_ref[...]


def _mod_call(c_all, mod_w, mod_b):
    depth, d, nm = mod_w.shape
    rows = c_all.shape[0]
    tn = 1152
    assert nm % tn == 0
    return pl.pallas_call(
        _mod_kernel,
        grid=(depth, nm // tn),
        in_specs=[
            pl.BlockSpec((rows, d), lambda i, j: (0, 0)),
            pl.BlockSpec((None, d, tn), lambda i, j: (i, 0, j)),
            pl.BlockSpec((None, 1, tn), lambda i, j: (i, 0, j)),
        ],
        out_specs=pl.BlockSpec((None, rows, tn), lambda i, j: (i, 0, j)),
        out_shape=jax.ShapeDtypeStruct((depth, rows, nm), F32),
        compiler_params=pltpu.CompilerParams(
            dimension_semantics=("arbitrary", "arbitrary"), vmem_limit_bytes=VMEM_LIMIT),
        name="adaln_mod",
    )(c_all, mod_w, mod_b.reshape(depth, 1, nm))


def _ffn_kernel(x_ref, mod_ref, g_ref, w1_ref, w3_ref, w2_ref, *rest, mod_row, n_chunks, final):
    if final:
        fg_ref, o_ref, hid_ref = rest
    else:
        o_ref, hid_ref = rest
    x = x_ref[...]
    sh = mod_ref[mod_row]
    sc = mod_ref[mod_row + 1]
    gt = mod_ref[mod_row + 2]
    hb = _modulated_rms(x, g_ref[...], sc, sh).astype(BF16)
    fc = w1_ref.shape[1] // n_chunks
    for c in range(n_chunks):
        a = _dot(hb, w1_ref[:, c * fc:(c + 1) * fc])
        b = _dot(hb, w3_ref[:, c * fc:(c + 1) * fc])
        hid_ref[:, c * fc:(c + 1) * fc] = (_silu(a) * b).astype(BF16)
    y = x + (0.5 * gt) * _dot(hid_ref[...], w2_ref[...])
    if final:
        y = y * lax.rsqrt(jnp.mean(y * y, axis=-1, keepdims=True) + EPS) * fg_ref[...]
    o_ref[...] = y


def _ffn_call(x, mod, layer, boff, mod_row, norm_g, gi, w1, w3, w2, li, lj, tm, final_g=None):
    b, s, d = x.shape
    dff = w1.shape[-1]
    n_chunks = dff // (2 * LANES)
    mod_r = mod.shape[3]
    assert s % tm == 0 and dff % (2 * LANES) == 0 and (mod_r == 1 or mod_r == tm == s)
    in_specs = [
        pl.BlockSpec((None, tm, d), lambda i, j: (i, j, 0)),
        pl.BlockSpec((None, None, N_MOD, mod_r, d), lambda i, j: (layer, i + boff, 0, 0, 0)),
        pl.BlockSpec((None, 1, d), lambda i, j: (gi, 0, 0)),
        _resident((None, None, d, dff), lambda i, j: (li, lj, 0, 0)),
        _resident((None, None, d, dff), lambda i, j: (li, lj, 0, 0)),
        _resident((None, None, dff, d), lambda i, j: (li, lj, 0, 0)),
    ]
    args = [x, mod, norm_g, w1, w3, w2]
    if final_g is not None:
        in_specs.append(pl.BlockSpec((1, d), lambda i, j: (0, 0)))
        args.append(final_g)
    return pl.pallas_call(
        functools.partial(_ffn_kernel, mod_row=mod_row, n_chunks=n_chunks, final=final_g is not None),
        grid=(b, s // tm),
        in_specs=in_specs,
        out_specs=pl.BlockSpec((None, tm, d), lambda i, j: (i, j, 0)),
        out_shape=jax.ShapeDtypeStruct((b, s, d), F32),
        scratch_shapes=[pltpu.VMEM((tm, dff), BF16)],
        compiler_params=pltpu.CompilerParams(
            dimension_semantics=("arbitrary", "arbitrary"), vmem_limit_bytes=VMEM_LIMIT),
        name="swiglu_ffn",
    )(*args)


CONV_ROWS = 64


def _shift_rows(e, r):
    return pltpu.roll(e, e.shape[0] - r, axis=0)


def _aligned(i):
    return pl.multiple_of(i, SUBLANES)


def _cmod_kernel(x_ref, mod_ref, g_ref, wp1_ref, bp1_ref, dww_ref, dwb_ref, lng_ref, lnb_ref,
                 wp2_ref, bp2_ref, cache0_ref, o_ref, cache_ref, hb_ref, ext_ref, sh_ref, v_ref, *, tm, valid, pad):
    s = pl.program_id(1)
    taps = dww_ref.shape[0]
    dc = dww_ref.shape[1] * LANES
    pk = PACKED_ROWS

    @pl.when(s == 0)
    def _():
        ext_ref[0:pad, :] = cache0_ref[...]

    x = x_ref[...]
    sh = mod_ref[3:4, :]
    sc = mod_ref[4:5, :]
    gt = mod_ref[5:6, :]
    hb_ref[...] = _modulated_rms(x, g_ref[...], sc, sh).astype(BF16)
    keep = sh_ref.shape[2] - pk
    first = pad - (taps - 1)
    nq = CONV_ROWS // pk
    slab = 2 * LANES

    def glu_slab(s0):
        hb = hb_ref[...]
        ua = _dot(hb, wp1_ref[:, s0:s0 + slab]) + bp1_ref[:, s0:s0 + slab]
        ug = _dot(hb, wp1_ref[:, dc + s0:dc + s0 + slab]) + bp1_ref[:, dc + s0:dc + s0 + slab]
        ext_ref[pad:pad + tm, s0:s0 + slab] = ua * jax.nn.sigmoid(ug)

    def conv_slab(s0):
        for c0 in range(s0, s0 + slab, LANES):
            ct = c0 // LANES
            e = ext_ref[:, c0:c0 + LANES]
            for r in range(SUBLANES):
                er = e if r == 0 else _shift_rows(e, r)
                sh_ref[r, ct, :, :] = er.astype(BF16)
                sh_ref[r + SUBLANES, ct, 0:keep, :] = er[SUBLANES:SUBLANES + keep, :].astype(BF16)
            for r0 in range(0, tm, CONV_ROWS):
                accs = [None] * nq
                for k in range(taps):
                    a, r = divmod(first + k, pk)
                    wk = dww_ref[k, ct].astype(F32)
                    for q in range(nq):
                        lo = r0 + pk * (a + q)
                        term = wk * sh_ref[r, ct, lo:lo + pk, :].astype(F32)
                        accs[q] = term if accs[q] is None else accs[q] + term
                for q in range(nq):
                    v_ref[r0 + pk * q:r0 + pk * (q + 1), c0:c0 + LANES] = accs[q] + dwb_ref[:, c0:c0 + LANES]

    for s0 in range(0, dc, slab):
        glu_slab(s0)
    for s0 in range(0, dc, slab):
        @pl.when(s >= 0)
        def _():
            conv_slab(s0)

    cache_ref[...] = ext_ref[pad + valid - (taps - 1):pad + valid, :]
    ext_ref[0:pad, :] = ext_ref[tm:tm + pad, :]

    v = v_ref[...]
    vc = v - jnp.mean(v, axis=-1, keepdims=True)
    y = vc * lax.rsqrt(jnp.mean(vc * vc, axis=-1, keepdims=True) + EPS) * lng_ref[...] + lnb_ref[...]
    out = _dot(_silu(y).astype(BF16), wp2_ref[...]) + bp2_ref[...]
    o_ref[...] = x_ref[...] + gt * out


def _cmod_call(x, mod, layer, boff, norm_g, gi, wp1, bp1, dww, dwb, lng, lnb, wp2, bp2, cache0, j, tm, valid):
    b, s, d = x.shape
    taps, dc = dww.shape[-2:]
    pad = 32
    assert s % tm == 0 and tm % CONV_ROWS == 0 and taps - 1 <= pad and (valid == tm or s == tm)
    assert dc % LANES == 0
    row = lambda a: a.reshape(a.shape[0], 1, a.shape[-1])
    sub = lambda a: jnp.broadcast_to(a[..., None, :], a.shape[:-1] + (PACKED_ROWS, a.shape[-1]))
    dww_t = jnp.broadcast_to(dww.astype(BF16).reshape(dww.shape[0], taps, dc // LANES, 1, LANES),
                             (dww.shape[0], taps, dc // LANES, PACKED_ROWS, LANES))
    out, cache = pl.pallas_call(
        functools.partial(_cmod_kernel, tm=tm, valid=valid, pad=pad),
        grid=(b, s // tm),
        in_specs=[
            pl.BlockSpec((None, tm, d), lambda i, k: (i, k, 0)),
            pl.BlockSpec((None, None, N_MOD, d), lambda i, k: (layer, i + boff, 0, 0)),
            pl.BlockSpec((None, 1, d), lambda i, k: (gi, 0, 0)),
            _resident((None, d, 2 * dc), lambda i, k: (j, 0, 0)),
            pl.BlockSpec((None, 1, 2 * dc), lambda i, k: (j, 0, 0)),
            pl.BlockSpec((None, taps, dc // LANES, PACKED_ROWS, LANES), lambda i, k: (j, 0, 0, 0, 0)),
            pl.BlockSpec((None, PACKED_ROWS, dc), lambda i, k: (j, 0, 0)),
            pl.BlockSpec((None, 1, dc), lambda i, k: (j, 0, 0)),
            pl.BlockSpec((None, 1, dc), lambda i, k: (j, 0, 0)),
            _resident((None, dc, d), lambda i, k: (j, 0, 0)),
            pl.BlockSpec((None, 1, d), lambda i, k: (j, 0, 0)),
            pl.BlockSpec((None, pad, dc), lambda i, k: (i, 0, 0)),
        ],
        out_specs=[
            pl.BlockSpec((None, tm, d), lambda i, k: (i, k, 0)),
            pl.BlockSpec((None, taps - 1, dc), lambda i, k: (i, 0, 0)),
        ],
        out_shape=[
            jax.ShapeDtypeStruct((b, s, d), F32),
            jax.ShapeDtypeStruct((b, taps - 1, dc), F32),
        ],
        scratch_shapes=[
            pltpu.VMEM((tm, d), BF16),
            pltpu.VMEM((tm + pad, dc), F32),
            pltpu.VMEM((PACKED_ROWS, dc // LANES, tm + pad, LANES), BF16),
            pltpu.VMEM((tm, dc), F32),
        ],
        compiler_params=pltpu.CompilerParams(
            dimension_semantics=("arbitrary", "arbitrary"), vmem_limit_bytes=VMEM_LIMIT),
        name="conformer_conv",
    )(x, mod, norm_g, wp1, row(bp1), dww_t, sub(dwb), row(lng), row(lnb), wp2, row(bp2), cache0)
    return out, cache


def _ssd_kernel(x_ref, mod_ref, g_ref, win_ref, cw_ref, cb_ref, dtb_ref, alog_ref, dexp_ref, ng_ref,
                wout_ref, ehp_ref, ehs_ref, cache0_ref, st0_ref,
                o_ref, cache_ref, st_ref,
                hb_ref, ext_ref, act_ref, dtr_ref, z_ref, *, tm, valid, di, gn, n, heads):
    s = pl.program_id(1)
    taps = cw_ref.shape[0]
    cd = cw_ref.shape[2]
    pad = SUBLANES
    lc = SSD_CHUNK
    groups = gn // n
    hg = heads // groups
    p = di // heads
    gw = hg * p

    @pl.when(s == 0)
    def _():
        ext_ref[0:pad, :] = cache0_ref[...]
        st_ref[...] = st0_ref[...]

    sh = mod_ref[3:4, :]
    sc = mod_ref[4:5, :]
    gt = mod_ref[5:6, :]
    hb_ref[...] = _modulated_rms(x_ref[...], g_ref[...], sc, sh).astype(BF16)

    first = pad - (taps - 1)
    nq = CONV_ROWS // SUBLANES
    slab = 4 * LANES
    neg_a = -jnp.exp(alog_ref[...]) * LOG2E
    causal = (lax.broadcasted_iota(jnp.int32, (lc, lc), 0) >= lax.broadcasted_iota(jnp.int32, (lc, lc), 1))
    tril = jnp.where(causal, 1.0, 0.0).astype(BF16)
    left = lax.broadcasted_iota(jnp.int32, (lc, 2 * p), 1) < p

    dtr_ref[...] = _dot(hb_ref[...], win_ref[:, di + cd:di + cd + LANES])
    gate_slabs = list(range(0, di, slab))
    for s0 in range(0, cd, slab):
        ext_ref[pad:pad + tm, s0:s0 + slab] = _dot(hb_ref[...], win_ref[:, di + s0:di + s0 + slab])
        for c0 in range(s0, s0 + slab, LANES):
            for rb in range(0, tm, CONV_ROWS):
                e = ext_ref[rb:rb + CONV_ROWS + pad, c0:c0 + LANES]
                accs = [cb_ref[:, c0:c0 + LANES]] * nq
                for k in range(taps):
                    a, r = divmod(first + k, SUBLANES)
                    wk = cw_ref[k, :, c0:c0 + LANES]
                    ek = e if r == 0 else _shift_rows(e, r)
                    for q in range(nq):
                        accs[q] = accs[q] + wk * ek[SUBLANES * (a + q):SUBLANES * (a + q + 1), :]
                for q in range(nq):
                    act_ref[rb + SUBLANES * q:rb + SUBLANES * (q + 1), c0:c0 + LANES] = _silu(accs[q])
        if gate_slabs:
            z0 = gate_slabs.pop(0)
            z_ref[:, z0:z0 + slab] = _dot(hb_ref[...], win_ref[:, z0:z0 + slab])
    for z0 in gate_slabs:
        z_ref[:, z0:z0 + slab] = _dot(hb_ref[...], win_ref[:, z0:z0 + slab])

    for c in range(tm // lc):
        r0 = c * lc
        v = dtr_ref[r0:r0 + lc, :] + dtb_ref[...]
        dt = jnp.maximum(v, 0.0) + jnp.log1p(jnp.exp(-jnp.abs(v)))
        if valid < r0 + lc:
            dt = jnp.where(lax.broadcasted_iota(jnp.int32, dt.shape, 0) < valid - r0, dt, 0.0)
        da = dt * neg_a
        d_hi = da.astype(BF16)
        d_r = da - d_hi.astype(F32)
        d_mid = d_r.astype(BF16)
        d_lo = (d_r - d_mid.astype(F32)).astype(BF16)
        cs = _dot(tril, jnp.concatenate([d_hi, d_mid, d_lo], axis=1))
        ac = _round16(cs[:, 0:LANES] + cs[:, LANES:2 * LANES] + cs[:, 2 * LANES:3 * LANES])
        row_t = ac.T - jnp.log2(dt.T)

        al = ac[lc - 1:lc, :]
        stack = jnp.concatenate(
            [jnp.exp2(ac), jnp.exp2(al - ac) * dt, jnp.broadcast_to(jnp.exp2(al), (SUBLANES, LANES))], axis=0)
        ex = _dot(_split2(stack), ehp_ref[...])
        eac_x = ex[0:lc, :]
        wend_x = ex[lc:2 * lc, :]
        dec_x = ex[2 * lc:2 * lc + 1, :]
        xs = act_ref[r0:r0 + lc, 0:di]
        xw = (xs * wend_x).astype(BF16)
        ac2 = _split2(ac)
        y_parts = []
        for g in range(groups):
            bg = act_ref[r0:r0 + lc, di + g * n:di + (g + 1) * n]
            cgb = act_ref[r0:r0 + lc, di + gn + g * n:di + gn + (g + 1) * n].astype(BF16)
            scores = lax.dot_general(cgb, bg.astype(BF16), (((1,), (1,)), ((), ())),
                                     preferred_element_type=F32)
            h_g = st_ref[:, g * gw:(g + 1) * gw]
            y_int = _dot(cgb, h_g.astype(BF16))
            colb = _dot(ac2, ehs_ref[:, g * hg * lc:(g + 1) * hg * lc])
            ys = []
            for jp in range(hg // 2):
                wl = []
                for j in (2 * jp, 2 * jp + 1):
                    hh = g * hg + j
                    seg = colb[:, j * lc:(j + 1) * lc] - row_t[hh:hh + 1, :]
                    decay = jnp.exp2(jnp.where(causal, seg, -jnp.inf))
                    wl.append((scores * decay).astype(BF16))
                w2 = jnp.concatenate(wl, axis=1)
                xp = xs[:, (g * hg + 2 * jp) * p:(g * hg + 2 * jp + 2) * p]
                x2 = jnp.concatenate([jnp.where(left, xp, 0.0), jnp.where(left, 0.0, xp)],
                                     axis=0).astype(BF16)
                ys.append(_dot(w2, x2))
            y_parts.append(jnp.concatenate(ys, axis=1) + y_int * eac_x[:, g * gw:(g + 1) * gw])
            st_ref[:, g * gw:(g + 1) * gw] = (h_g * dec_x[:, g * gw:(g + 1) * gw]
                                              + _dot(bg.T.astype(BF16), xw[:, g * gw:(g + 1) * gw]))
        y = jnp.concatenate(y_parts, axis=1) + dexp_ref[...] * xs
        yv = y * _silu(z_ref[r0:r0 + lc, :])
        yn = yv * lax.rsqrt(jnp.mean(yv * yv, axis=-1, keepdims=True) + EPS) * ng_ref[...]
        o_ref[r0:r0 + lc, :] = x_ref[r0:r0 + lc, :] + gt * _dot(yn.astype(BF16), wout_ref[...])

    cache_ref[...] = ext_ref[pad + valid - (taps - 1):pad + valid, :]
    ext_ref[0:pad, :] = ext_ref[tm:tm + pad, :]


def _ssd_call(x, mod, layer, boff, norm_g, gi, win, cw, cb, dtb, alog, dexp, ng, wout, ehp, ehs,
              cache0, st0, j, tm, valid, heads, n):
    b, s, d = x.shape
    taps, cd = cw.shape[-2:]
    di = wout.shape[-2]
    gn = (cd - di) // 2
    pw = win.shape[-1]
    assert s % tm == 0 and tm % SSD_CHUNK == 0 and tm % CONV_ROWS == 0 and (valid == tm or s == tm)
    row = lambda a: a.reshape(a.shape[0], 1, a.shape[-1])
    sub = lambda a: jnp.broadcast_to(a[..., None, :], a.shape[:-1] + (SUBLANES, a.shape[-1]))
    return pl.pallas_call(
        functools.partial(_ssd_kernel, tm=tm, valid=valid, di=di, gn=gn, n=n, heads=heads),
        grid=(b, s // tm),
        in_specs=[
            pl.BlockSpec((None, tm, d), lambda i, k: (i, k, 0)),
            pl.BlockSpec((None, None, N_MOD, d), lambda i, k: (layer, i + boff, 0, 0)),
            pl.BlockSpec((None, 1, d), lambda i, k: (gi, 0, 0)),
            _resident((None, d, pw), lambda i, k: (j, 0, 0)),
            pl.BlockSpec((None, taps, SUBLANES, cd), lambda i, k: (j, 0, 0, 0)),
            pl.BlockSpec((None, SUBLANES, cd), lambda i, k: (j, 0, 0)),
            pl.BlockSpec((None, 1, LANES), lambda i, k: (j, 0, 0)),
            pl.BlockSpec((None, 1, LANES), lambda i, k: (j, 0, 0)),
            pl.BlockSpec((None, 1, di), lambda i, k: (j, 0, 0)),
            pl.BlockSpec((None, 1, di), lambda i, k: (j, 0, 0)),
            _resident((None, di, d), lambda i, k: (j, 0, 0)),
            _resident(ehp.shape, lambda i, k: (0, 0)),
            _resident(ehs.shape, lambda i, k: (0, 0)),
            pl.BlockSpec((None, SUBLANES, cd), lambda i, k: (i, 0, 0)),
            pl.BlockSpec((None, n, di), lambda i, k: (i, 0, 0)),
        ],
        out_specs=[
            pl.BlockSpec((None, tm, d), lambda i, k: (i, k, 0)),
            pl.BlockSpec((None, taps - 1, cd), lambda i, k: (i, 0, 0)),
            pl.BlockSpec((None, n, di), lambda i, k: (i, 0, 0)),
        ],
        out_shape=[
            jax.ShapeDtypeStruct((b, s, d), F32),
            jax.ShapeDtypeStruct((b, taps - 1, cd), F32),
            jax.ShapeDtypeStruct((b, n, di), F32),
        ],
        scratch_shapes=[
            pltpu.VMEM((tm, d), BF16),
            pltpu.VMEM((tm + SUBLANES, cd), F32),
            pltpu.VMEM((tm, cd), F32),
            pltpu.VMEM((tm, LANES), F32),
            pltpu.VMEM((tm, di), F32),
        ],
        compiler_params=pltpu.CompilerParams(
            dimension_semantics=("arbitrary", "arbitrary"), vmem_limit_bytes=VMEM_LIMIT),
        name="ssd_mixer",
    )(x, mod, norm_g, win, sub(cw), sub(cb), row(dtb), row(alog), row(dexp), row(ng), wout, ehp, ehs, cache0, st0)


def _head_expanders(heads, p, lc):
    h_of_row = jnp.arange(2 * LANES) % LANES
    ehp = (h_of_row[:, None] == (jnp.arange(heads * p) // p)[None, :]).astype(BF16)
    ehs = (h_of_row[:, None] == (jnp.arange(heads * lc) // lc)[None, :]).astype(BF16)
    return ehp, ehs


def _tiles(seq):
    pick = lambda cap: max(t for t in (SSD_CHUNK, 2 * SSD_CHUNK, 4 * SSD_CHUNK, 8 * SSD_CHUNK)
                           if t <= cap and seq % t == 0)
    return pick(1024), pick(512), pick(512)


def _trunk(x, mod, boff, valid, ssd_state_t, ssd_conv_cache, cmod_conv_cache, wts, flat_ffn):
    (norm_g, w1, w3, w2, win, cw, cb, dtb, alog, dexp, ssd_ng, wout, ehp, ehs,
     wp1, bp1, dww, dwb, lng, lnb, wp2, bp2, final_g, heads, n) = wts
    depth = norm_g.shape[0] // 3
    b, seq, d = x.shape
    tm_ffn, tm_ssd, tm_conv = _tiles(seq)
    nb = mod.shape[1]
    if flat_ffn:
        rows = b * valid
        mod_ffn = jnp.repeat(jnp.swapaxes(mod[:, boff:boff + b], 1, 2), valid, axis=2)[:, None]
    else:
        mod_ffn = mod.reshape(depth, nb, N_MOD, 1, d)

    def ffn(x, i, mod_row, gi, lj, final=None):
        if not flat_ffn:
            return _ffn_call(x, mod_ffn, i, boff, mod_row, norm_g, gi, w1, w3, w2, i, lj, tm_ffn, final_g=final)
        y = _ffn_call(x[:, :valid].reshape(1, rows, d), mod_ffn, i, 0, mod_row, norm_g, gi, w1, w3, w2, i, lj,
                      rows, final_g=final)
        return jnp.pad(y.reshape(b, valid, d), ((0, 0), (0, seq - valid), (0, 0)))

    new_state, new_ssd_conv, new_cmod_conv = [], [], []
    for i in range(depth):
        x = ffn(x, i, 0, 3 * i, 0)
        j = i // 2
        if i % 2 == 0:
            x, cache, st = _ssd_call(x, mod, i, boff, norm_g, 3 * i + 1, win, cw, cb, dtb, alog, dexp, ssd_ng,
                                     wout, ehp, ehs, ssd_conv_cache[j], ssd_state_t[j], j, tm_ssd,
                                     min(valid, tm_ssd), heads, n)
            new_ssd_conv.append(cache)
            new_state.append(st)
        else:
            x, cache = _cmod_call(x, mod, i, boff, norm_g, 3 * i + 1, wp1, bp1, dww, dwb, lng, lnb, wp2, bp2,
                                  cmod_conv_cache[j], j, tm_conv, min(valid, tm_conv))
            new_cmod_conv.append(cache)
        x = ffn(x, i, 6, 3 * i + 2, 1, final_g if i == depth - 1 else None)
    return x, jnp.stack(new_state), jnp.stack(new_ssd_conv), jnp.stack(new_cmod_conv)


def kernel(x_prompt, x_sample, c_prompt, c_sample, state_ssd, cache_ssd_conv, cache_cmod_conv, mod_w, mod_b, norm_g, ffn_w1, ffn_w3, ffn_w2, ssd_w_in, ssd_conv_w, ssd_conv_b, ssd_dt_bias, ssd_a_log, ssd_d, ssd_norm_g, ssd_w_out, cmod_w_pw1, cmod_b_pw1, cmod_dw_w, cmod_dw_b, cmod_ln_g, cmod_ln_b, cmod_w_pw2, cmod_b_pw2, final_g):
    bp, sp, d = x_prompt.shape
    bs, ss, _ = x_sample.shape
    n_ssd, _, heads, p, n = state_ssd.shape
    di = heads * p
    cd = ssd_conv_w.shape[-1]
    depth = mod_w.shape[0]
    assert heads <= LANES and ss <= SSD_CHUNK

    mod = _mod_call(jnp.concatenate([c_prompt, c_sample], axis=0), mod_w, mod_b)
    mod = mod.reshape(depth, bp + bs, N_MOD, d)

    lane_pad = lambda a: jnp.pad(a, ((0, 0), (0, LANES - a.shape[-1])))
    win = jnp.concatenate(
        [ssd_w_in[..., :di + cd], jnp.pad(ssd_w_in[..., di + cd:], ((0, 0), (0, 0), (0, LANES - heads)))],
        axis=-1).astype(BF16)
    ehp, ehs = _head_expanders(heads, p, SSD_CHUNK)
    wts = (norm_g.reshape(depth * 3, 1, d), ffn_w1.astype(BF16), ffn_w3.astype(BF16), ffn_w2.astype(BF16),
           win, ssd_conv_w, ssd_conv_b, lane_pad(ssd_dt_bias), lane_pad(ssd_a_log),
           jnp.repeat(ssd_d, p, axis=-1), ssd_norm_g, ssd_w_out.astype(BF16), ehp, ehs,
           cmod_w_pw1.astype(BF16), cmod_b_pw1, cmod_dw_w, cmod_dw_b, cmod_ln_g, cmod_ln_b,
           cmod_w_pw2.astype(BF16), cmod_b_pw2, final_g.reshape(1, d), heads, n)

    def to_t(st):
        return jnp.swapaxes(st.reshape(st.shape[0], st.shape[1], di, n), 2, 3)

    def from_t(st_t):
        return jnp.swapaxes(st_t, 2, 3).reshape(st_t.shape[0], st_t.shape[1], heads, p, n)

    cmod_pad = 32 - cache_cmod_conv.shape[2]
    ssd_pad = SUBLANES - cache_ssd_conv.shape[2]

    zeros = lambda *shape: jnp.zeros(shape, F32)
    y_p, st_p, sc_p, cc_p = _trunk(
        x_prompt, mod, 0, sp,
        zeros(n_ssd, bp, n, di), zeros(n_ssd, bp, SUBLANES, cd),
        zeros(cache_cmod_conv.shape[0], bp, 32, cache_cmod_conv.shape[-1]), wts, flat_ffn=False)

    xs_pad = jnp.pad(x_sample, ((0, 0), (0, SSD_CHUNK - ss), (0, 0)))
    y_s, st_s, sc_s, cc_s = _trunk(
        xs_pad, mod, bp, ss,
        to_t(state_ssd), jnp.pad(cache_ssd_conv, ((0, 0), (0, 0), (ssd_pad, 0), (0, 0))),
        jnp.pad(cache_cmod_conv, ((0, 0), (0, 0), (cmod_pad, 0), (0, 0))), wts,
        flat_ffn=(bs * ss) % SUBLANES == 0)

    return (y_p, y_s[:, :ss], from_t(st_p), sc_p, cc_p, from_t(st_s), sc_s, cc_s)
```

```python
import functools

import jax
import jax.numpy as jnp
from jax import lax
from jax.experimental import pallas as pl
from jax.experimental.pallas import tpu as pltpu

F32 = jnp.float32
BF16 = jnp.bfloat16
EPS = 1e-6
LOG2E = 1.4426950408889634
N_MOD = 9
LANES = 128
SUBLANES = 8
PACKED_ROWS = 16
SSD_CHUNK = 128
VMEM_LIMIT = 56 * 1024 * 1024


def _dot(a, b):
    return jnp.dot(a, b, preferred_element_type=F32)


def _silu(v):
    return v * jax.nn.sigmoid(v)


def _modulated_rms(x, g, sc, sh):
    y = x * lax.rsqrt(jnp.mean(x * x, axis=-1, keepdims=True) + EPS) * g
    return y * (1.0 + sc) + sh


def _split2(v):
    hi = v.astype(BF16)
    lo = (v - hi.astype(F32)).astype(BF16)
    return jnp.concatenate([hi, lo], axis=1)


def _round16(v):
    hi = v.astype(BF16)
    lo = (v - hi.astype(F32)).astype(BF16)
    return hi.astype(F32) + lo.astype(F32)


def _resident(shape, index_map):
    return pl.BlockSpec(shape, index_map, pipeline_mode=pl.Buffered(1))


def _mod_kernel(c_ref, w_ref, b_ref, o_ref):
    c = c_ref[...]
    o_ref[...] = _dot(_silu(c).astype(BF16), w_ref[...].astype(BF16)) + b_ref[...]


def _mod_call(c_all, mod_w, mod_b):
    depth, d, nm = mod_w.shape
    rows = c_all.shape[0]
    tn = 1152
    assert nm % tn == 0
    return pl.pallas_call(
        _mod_kernel,
        grid=(depth, nm // tn),
        in_specs=[
            pl.BlockSpec((rows, d), lambda i, j: (0, 0)),
            pl.BlockSpec((None, d, tn), lambda i, j: (i, 0, j)),
            pl.BlockSpec((None, 1, tn), lambda i, j: (i, 0, j)),
        ],
        out_specs=pl.BlockSpec((None, rows, tn), lambda i, j: (i, 0, j)),
        out_shape=jax.ShapeDtypeStruct((depth, rows, nm), F32),
        compiler_params=pltpu.CompilerParams(
            dimension_semantics=("arbitrary", "arbitrary"), vmem_limit_bytes=VMEM_LIMIT),
        name="adaln_mod",
    )(c_all, mod_w, mod_b.reshape(depth, 1, nm))


def _ffn_kernel(x_ref, mod_ref, g_ref, w1_ref, w3_ref, w2_ref, *rest, mod_row, n_chunks, final):
    if final:
        fg_ref, o_ref, hid_ref = rest
    else:
        o_ref, hid_ref = rest
    x = x_ref[...]
    sh = mod_ref[mod_row]
    sc = mod_ref[mod_row + 1]
    gt = mod_ref[mod_row + 2]
    hb = _modulated_rms(x, g_ref[...], sc, sh).astype(BF16)
    fc = w1_ref.shape[1] // n_chunks
    for c in range(n_chunks):
        a = _dot(hb, w1_ref[:, c * fc:(c + 1) * fc])
        b = _dot(hb, w3_ref[:, c * fc:(c + 1) * fc])
        hid_ref[:, c * fc:(c + 1) * fc] = (_silu(a) * b).astype(BF16)
    y = x + (0.5 * gt) * _dot(hid_ref[...], w2_ref[...])
    if final:
        y = y * lax.rsqrt(jnp.mean(y * y, axis=-1, keepdims=True) + EPS) * fg_ref[...]
    o_ref[...] = y


def _ffn_call(x, mod, layer, boff, mod_row, norm_g, gi, w1, w3, w2, li, lj, tm, final_g=None):
    b, s, d = x.shape
    dff = w1.shape[-1]
    n_chunks = dff // (2 * LANES)
    mod_r = mod.shape[3]
    assert s % tm == 0 and dff % (2 * LANES) == 0 and (mod_r == 1 or mod_r == tm == s)
    in_specs = [
        pl.BlockSpec((None, tm, d), lambda i, j: (i, j, 0)),
        pl.BlockSpec((None, None, N_MOD, mod_r, d), lambda i, j: (layer, i + boff, 0, 0, 0)),
        pl.BlockSpec((None, 1, d), lambda i, j: (gi, 0, 0)),
        _resident((None, None, d, dff), lambda i, j: (li, lj, 0, 0)),
        _resident((None, None, d, dff), lambda i, j: (li, lj, 0, 0)),
        _resident((None, None, dff, d), lambda i, j: (li, lj, 0, 0)),
    ]
    args = [x, mod, norm_g, w1, w3, w2]
    if final_g is not None:
        in_specs.append(pl.BlockSpec((1, d), lambda i, j: (0, 0)))
        args.append(final_g)
    return pl.pallas_call(
        functools.partial(_ffn_kernel, mod_row=mod_row, n_chunks=n_chunks, final=final_g is not None),
        grid=(b, s // tm),
        in_specs=in_specs,
        out_specs=pl.BlockSpec((None, tm, d), lambda i, j: (i, j, 0)),
        out_shape=jax.ShapeDtypeStruct((b, s, d), F32),
        scratch_shapes=[pltpu.VMEM((tm, dff), BF16)],
        compiler_params=pltpu.CompilerParams(
            dimension_semantics=("arbitrary", "arbitrary"), vmem_limit_bytes=VMEM_LIMIT),
        name="swiglu_ffn",
    )(*args)


CONV_ROWS = 64


def _shift_rows(e, r):
    return pltpu.roll(e, e.shape[0] - r, axis=0)


def _aligned(i):
    return pl.multiple_of(i, SUBLANES)


def _cmod_kernel(x_ref, mod_ref, g_ref, wp1_ref, bp1_ref, dww_ref, dwb_ref, lng_ref, lnb_ref,
                 wp2_ref, bp2_ref, cache0_ref, o_ref, cache_ref, hb_ref, ext_ref, sh_ref, v_ref, *, tm, valid, pad):
    s = pl.program_id(1)
    taps = dww_ref.shape[0]
    dc = dww_ref.shape[1] * LANES
    pk = PACKED_ROWS

    @pl.when(s == 0)
    def _():
        ext_ref[0:pad, :] = cache0_ref[...]

    x = x_ref[...]
    sh = mod_ref[3:4, :]
    sc = mod_ref[4:5, :]
    gt = mod_ref[5:6, :]
    hb_ref[...] = _modulated_rms(x, g_ref[...], sc, sh).astype(BF16)
    keep = sh_ref.shape[2] - pk
    first = pad - (taps - 1)
    nq = CONV_ROWS // pk
    slab = 2 * LANES

    def glu_slab(s0):
        hb = hb_ref[...]
        ua = _dot(hb, wp1_ref[:, s0:s0 + slab]) + bp1_ref[:, s0:s0 + slab]
        ug = _dot(hb, wp1_ref[:, dc + s0:dc + s0 + slab]) + bp1_ref[:, dc + s0:dc + s0 + slab]
        ext_ref[pad:pad + tm, s0:s0 + slab] = ua * jax.nn.sigmoid(ug)

    def conv_slab(s0):
        for c0 in range(s0, s0 + slab, LANES):
            ct = c0 // LANES
            e = ext_ref[:, c0:c0 + LANES]
            for r in range(SUBLANES):
                er = e if r == 0 else _shift_rows(e, r)
                sh_ref[r, ct, :, :] = er.astype(BF16)
                sh_ref[r + SUBLANES, ct, 0:keep, :] = er[SUBLANES:SUBLANES + keep, :].astype(BF16)
            for r0 in range(0, tm, CONV_ROWS):
                accs = [None] * nq
                for k in range(taps):
                    a, r = divmod(first + k, pk)
                    wk = dww_ref[k, ct].astype(F32)
                    for q in range(nq):
                        lo = r0 + pk * (a + q)
                        term = wk * sh_ref[r, ct, lo:lo + pk, :].astype(F32)
                        accs[q] = term if accs[q] is None else accs[q] + term
                for q in range(nq):
                    v_ref[r0 + pk * q:r0 + pk * (q + 1), c0:c0 + LANES] = accs[q] + dwb_ref[:, c0:c0 + LANES]

    for s0 in range(0, dc, slab):
        glu_slab(s0)
    for s0 in range(0, dc, slab):
        @pl.when(s >= 0)
        def _():
            conv_slab(s0)

    cache_ref[...] = ext_ref[pad + valid - (taps - 1):pad + valid, :]
    ext_ref[0:pad, :] = ext_ref[tm:tm + pad, :]

    v = v_ref[...]
    vc = v - jnp.mean(v, axis=-1, keepdims=True)
    y = vc * lax.rsqrt(jnp.mean(vc * vc, axis=-1, keepdims=True) + EPS) * lng_ref[...] + lnb_ref[...]
    out = _dot(_silu(y).astype(BF16), wp2_ref[...]) + bp2_ref[...]
    o_ref[...] = x_ref[...] + gt * out


def _cmod_call(x, mod, layer, boff, norm_g, gi, wp1, bp1, dww, dwb, lng, lnb, wp2, bp2, cache0, j, tm, valid):
    b, s, d = x.shape
    taps, dc = dww.shape[-2:]
    pad = 32
    assert s % tm == 0 and tm % CONV_ROWS == 0 and taps - 1 <= pad and (valid == tm or s == tm)
    assert dc % LANES == 0
    row = lambda a: a.reshape(a.shape[0], 1, a.shape[-1])
    sub = lambda a: jnp.broadcast_to(a[..., None, :], a.shape[:-1] + (PACKED_ROWS, a.shape[-1]))
    dww_t = jnp.broadcast_to(dww.astype(BF16).reshape(dww.shape[0], taps, dc // LANES, 1, LANES),
                             (dww.shape[0], taps, dc // LANES, PACKED_ROWS, LANES))
    out, cache = pl.pallas_call(
        functools.partial(_cmod_kernel, tm=tm, valid=valid, pad=pad),
        grid=(b, s // tm),
        in_specs=[
            pl.BlockSpec((None, tm, d), lambda i, k: (i, k, 0)),
            pl.BlockSpec((None, None, N_MOD, d), lambda i, k: (layer, i + boff, 0, 0)),
            pl.BlockSpec((None, 1, d), lambda i, k: (gi, 0, 0)),
            _resident((None, d, 2 * dc), lambda i, k: (j, 0, 0)),
            pl.BlockSpec((None, 1, 2 * dc), lambda i, k: (j, 0, 0)),
            pl.BlockSpec((None, taps, dc // LANES, PACKED_ROWS, LANES), lambda i, k: (j, 0, 0, 0, 0)),
            pl.BlockSpec((None, PACKED_ROWS, dc), lambda i, k: (j, 0, 0)),
            pl.BlockSpec((None, 1, dc), lambda i, k: (j, 0, 0)),
            pl.BlockSpec((None, 1, dc), lambda i, k: (j, 0, 0)),
            _resident((None, dc, d), lambda i, k: (j, 0, 0)),
            pl.BlockSpec((None, 1, d), lambda i, k: (j, 0, 0)),
            pl.BlockSpec((None, pad, dc), lambda i, k: (i, 0, 0)),
        ],
        out_specs=[
            pl.BlockSpec((None, tm, d), lambda i, k: (i, k, 0)),
            pl.BlockSpec((None, taps - 1, dc), lambda i, k: (i, 0, 0)),
        ],
        out_shape=[
            jax.ShapeDtypeStruct((b, s, d), F32),
            jax.ShapeDtypeStruct((b, taps - 1, dc), F32),
        ],
        scratch_shapes=[
            pltpu.VMEM((tm, d), BF16),
            pltpu.VMEM((tm + pad, dc), F32),
            pltpu.VMEM((PACKED_ROWS, dc // LANES, tm + pad, LANES), BF16),
            pltpu.VMEM((tm, dc), F32),
        ],
        compiler_params=pltpu.CompilerParams(
            dimension_semantics=("arbitrary", "arbitrary"), vmem_limit_bytes=VMEM_LIMIT),
        name="conformer_conv",
    )(x, mod, norm_g, wp1, row(bp1), dww_t, sub(dwb), row(lng), row(lnb), wp2, row(bp2), cache0)
    return out, cache


def _ssd_kernel(x_ref, mod_ref, g_ref, win_ref, wdt_ref, cw_ref, cb_ref, dtb_ref, alog_ref, dexp_ref, ng_ref,
                wout_ref, ehp_ref, ehs_ref, cache0_ref, st0_ref,
                o_ref, cache_ref, st_ref,
                hb_ref, ext_ref, act_ref, dtr_ref, z_ref, *, tm, valid, di, gn, n, heads):
    s = pl.program_id(1)
    taps = cw_ref.shape[0]
    cd = cw_ref.shape[2]
    pad = SUBLANES
    lc = SSD_CHUNK
    groups = gn // n
    hg = heads // groups
    p = di // heads
    gw = hg * p

    @pl.when(s == 0)
    def _():
        ext_ref[0:pad, :] = cache0_ref[...]
        st_ref[...] = st0_ref[...]

    sh = mod_ref[3:4, :]
    sc = mod_ref[4:5, :]
    gt = mod_ref[5:6, :]
    hb_ref[...] = _modulated_rms(x_ref[...], g_ref[...], sc, sh).astype(BF16)

    first = pad - (taps - 1)
    nq = CONV_ROWS // SUBLANES
    slab = 2 * LANES
    neg_a = -jnp.exp(alog_ref[...]) * LOG2E
    causal = (lax.broadcasted_iota(jnp.int32, (lc, lc), 0) >= lax.broadcasted_iota(jnp.int32, (lc, lc), 1))
    tril = jnp.where(causal, 1.0, 0.0).astype(BF16)
    left = lax.broadcasted_iota(jnp.int32, (lc, 2 * p), 1) < p

    dtr_ref[...] = _dot(hb_ref[...], wdt_ref[...])
    gate_slabs = list(range(0, di, slab))
    for s0 in range(0, cd, slab):
        ext_ref[pad:pad + tm, s0:s0 + slab] = _dot(hb_ref[...], win_ref[:, di + s0:di + s0 + slab])
        for c0 in range(s0, s0 + slab, LANES):
            for rb in range(0, tm, CONV_ROWS):
                e = ext_ref[rb:rb + CONV_ROWS + pad, c0:c0 + LANES]
                accs = [cb_ref[:, c0:c0 + LANES]] * nq
                for k in range(taps):
                    a, r = divmod(first + k, SUBLANES)
                    wk = cw_ref[k, :, c0:c0 + LANES]
                    ek = e if r == 0 else _shift_rows(e, r)
                    for q in range(nq):
                        accs[q] = accs[q] + wk * ek[SUBLANES * (a + q):SUBLANES * (a + q + 1), :]
                for q in range(nq):
                    act_ref[rb + SUBLANES * q:rb + SUBLANES * (q + 1), c0:c0 + LANES] = _silu(accs[q])
        if gate_slabs:
            z0 = gate_slabs.pop(0)
            z_ref[:, z0:z0 + slab] = _dot(hb_ref[...], win_ref[:, z0:z0 + slab])
    for z0 in gate_slabs:
        z_ref[:, z0:z0 + slab] = _dot(hb_ref[...], win_ref[:, z0:z0 + slab])

    for c in range(tm // lc):
        r0 = c * lc
        v = dtr_ref[r0:r0 + lc, :] + dtb_ref[...]
        dt = jnp.maximum(v, 0.0) + jnp.log1p(jnp.exp(-jnp.abs(v)))
        if valid < r0 + lc:
            dt = jnp.where(lax.broadcasted_iota(jnp.int32, dt.shape, 0) < valid - r0, dt, 0.0)
        da = dt * neg_a
        d_hi = da.astype(BF16)
        d_r = da - d_hi.astype(F32)
        d_mid = d_r.astype(BF16)
        d_lo = (d_r - d_mid.astype(F32)).astype(BF16)
        cs = _dot(tril, jnp.concatenate([d_hi, d_mid, d_lo], axis=1))
        ac = _round16(cs[:, 0:LANES] + cs[:, LANES:2 * LANES] + cs[:, 2 * LANES:3 * LANES])
        row_t = ac.T - jnp.log2(dt.T)

        al = ac[lc - 1:lc, :]
        stack = jnp.concatenate(
            [jnp.exp2(ac), jnp.exp2(al - ac) * dt, jnp.broadcast_to(jnp.exp2(al), (SUBLANES, LANES))], axis=0)
        ex = _dot(_split2(stack), ehp_ref[...])
        eac_x = ex[0:lc, :]
        wend_x = ex[lc:2 * lc, :]
        dec_x = ex[2 * lc:2 * lc + 1, :]
        xs = act_ref[r0:r0 + lc, 0:di]
        xw = (xs * wend_x).astype(BF16)
        ac2 = _split2(ac)
        y_parts = []
        for g in range(groups):
            bg = act_ref[r0:r0 + lc, di + g * n:di + (g + 1) * n]
            cgb = act_ref[r0:r0 + lc, di + gn + g * n:di + gn + (g + 1) * n].astype(BF16)
            scores = lax.dot_general(cgb, bg.astype(BF16), (((1,), (1,)), ((), ())),
                                     preferred_element_type=F32)
            h_g = st_ref[:, g * gw:(g + 1) * gw]
            y_int = _dot(cgb, h_g.astype(BF16))
            colb = _dot(ac2, ehs_ref[:, g * hg * lc:(g + 1) * hg * lc])
            ys = []
            for jp in range(hg // 2):
                wl = []
                for j in (2 * jp, 2 * jp + 1):
                    hh = g * hg + j
                    seg = colb[:, j * lc:(j + 1) * lc] - row_t[hh:hh + 1, :]
                    decay = jnp.exp2(jnp.where(causal, seg, -jnp.inf))
                    wl.append((scores * decay).astype(BF16))
                w2 = jnp.concatenate(wl, axis=1)
                xp = xs[:, (g * hg + 2 * jp) * p:(g * hg + 2 * jp + 2) * p]
                x2 = jnp.concatenate([jnp.where(left, xp, 0.0), jnp.where(left, 0.0, xp)],
                                     axis=0).astype(BF16)
                ys.append(_dot(w2, x2))
            y_parts.append(jnp.concatenate(ys, axis=1) + y_int * eac_x[:, g * gw:(g + 1) * gw])
            st_ref[:, g * gw:(g + 1) * gw] = (h_g * dec_x[:, g * gw:(g + 1) * gw]
                                              + _dot(bg.T.astype(BF16), xw[:, g * gw:(g + 1) * gw]))
        y = jnp.concatenate(y_parts, axis=1) + dexp_ref[...] * xs
        yv = y * _silu(z_ref[r0:r0 + lc, :])
        yn = yv * lax.rsqrt(jnp.mean(yv * yv, axis=-1, keepdims=True) + EPS) * ng_ref[...]
        o_ref[r0:r0 + lc, :] = x_ref[r0:r0 + lc, :] + gt * _dot(yn.astype(BF16), wout_ref[...])

    cache_ref[...] = ext_ref[pad + valid - (taps - 1):pad + valid, :]
    ext_ref[0:pad, :] = ext_ref[tm:tm + pad, :]


def _ssd_call(x, mod, layer, boff, norm_g, gi, win, wdt, cw, cb, dtb, alog, dexp, ng, wout, ehp, ehs,
              cache0, st0, j, tm, valid, heads, n):
    b, s, d = x.shape
    taps, cd = cw.shape[-2:]
    di = wout.shape[-2]
    gn = (cd - di) // 2
    pw = win.shape[-1]
    assert s % tm == 0 and tm % SSD_CHUNK == 0 and tm % CONV_ROWS == 0 and (valid == tm or s == tm)
    row = lambda a: a.reshape(a.shape[0], 1, a.shape[-1])
    sub = lambda a: jnp.broadcast_to(a[..., None, :], a.shape[:-1] + (SUBLANES, a.shape[-1]))
    return pl.pallas_call(
        functools.partial(_ssd_kernel, tm=tm, valid=valid, di=di, gn=gn, n=n, heads=heads),
        grid=(b, s // tm),
        in_specs=[
            pl.BlockSpec((None, tm, d), lambda i, k: (i, k, 0)),
            pl.BlockSpec((None, None, N_MOD, d), lambda i, k: (layer, i + boff, 0, 0)),
            pl.BlockSpec((None, 1, d), lambda i, k: (gi, 0, 0)),
            _resident((None, d, pw), lambda i, k: (j, 0, 0)),
            _resident((None, d, LANES), lambda i, k: (j, 0, 0)),
            pl.BlockSpec((None, taps, SUBLANES, cd), lambda i, k: (j, 0, 0, 0)),
            pl.BlockSpec((None, SUBLANES, cd), lambda i, k: (j, 0, 0)),
            pl.BlockSpec((None, 1, LANES), lambda i, k: (j, 0, 0)),
            pl.BlockSpec((None, 1, LANES), lambda i, k: (j, 0, 0)),
            pl.BlockSpec((None, 1, di), lambda i, k: (j, 0, 0)),
            pl.BlockSpec((None, 1, di), lambda i, k: (j, 0, 0)),
            _resident((None, di, d), lambda i, k: (j, 0, 0)),
            _resident(ehp.shape, lambda i, k: (0, 0)),
            _resident(ehs.shape, lambda i, k: (0, 0)),
            pl.BlockSpec((None, SUBLANES, cd), lambda i, k: (i, 0, 0)),
            pl.BlockSpec((None, n, di), lambda i, k: (i, 0, 0)),
        ],
        out_specs=[
            pl.BlockSpec((None, tm, d), lambda i, k: (i, k, 0)),
            pl.BlockSpec((None, taps - 1, cd), lambda i, k: (i, 0, 0)),
            pl.BlockSpec((None, n, di), lambda i, k: (i, 0, 0)),
        ],
        out_shape=[
            jax.ShapeDtypeStruct((b, s, d), F32),
            jax.ShapeDtypeStruct((b, taps - 1, cd), F32),
            jax.ShapeDtypeStruct((b, n, di), F32),
        ],
        scratch_shapes=[
            pltpu.VMEM((tm, d), BF16),
            pltpu.VMEM((tm + SUBLANES, cd), F32),
            pltpu.VMEM((tm, cd), F32),
            pltpu.VMEM((tm, LANES), F32),
            pltpu.VMEM((tm, di), F32),
        ],
        compiler_params=pltpu.CompilerParams(
            dimension_semantics=("arbitrary", "arbitrary"), vmem_limit_bytes=VMEM_LIMIT),
        name="ssd_mixer",
    )(x, mod, norm_g, win, wdt, sub(cw), sub(cb), row(dtb), row(alog), row(dexp), row(ng), wout, ehp, ehs, cache0, st0)


def _head_expanders(heads, p, lc):
    h_of_row = jnp.arange(2 * LANES) % LANES
    ehp = (h_of_row[:, None] == (jnp.arange(heads * p) // p)[None, :]).astype(BF16)
    ehs = (h_of_row[:, None] == (jnp.arange(heads * lc) // lc)[None, :]).astype(BF16)
    return ehp, ehs


def _tiles(seq):
    pick = lambda cap: max(t for t in (SSD_CHUNK, 2 * SSD_CHUNK, 4 * SSD_CHUNK, 8 * SSD_CHUNK)
                           if t <= cap and seq % t == 0)
    return pick(1024), pick(512), pick(512)


def _trunk(x, mod, boff, valid, ssd_state_t, ssd_conv_cache, cmod_conv_cache, wts, flat_ffn):
    (norm_g, w1, w3, w2, win, wdt, cw, cb, dtb, alog, dexp, ssd_ng, wout, ehp, ehs,
     wp1, bp1, dww, dwb, lng, lnb, wp2, bp2, final_g, heads, n) = wts
    depth = norm_g.shape[0] // 3
    b, seq, d = x.shape
    tm_ffn, tm_ssd, tm_conv = _tiles(seq)
    nb = mod.shape[1]
    if flat_ffn:
        rows = b * valid
        mod_ffn = jnp.repeat(jnp.swapaxes(mod[:, boff:boff + b], 1, 2), valid, axis=2)[:, None]
    else:
        mod_ffn = mod.reshape(depth, nb, N_MOD, 1, d)

    def ffn(x, i, mod_row, gi, lj, final=None):
        if not flat_ffn:
            return _ffn_call(x, mod_ffn, i, boff, mod_row, norm_g, gi, w1, w3, w2, i, lj, tm_ffn, final_g=final)
        y = _ffn_call(x[:, :valid].reshape(1, rows, d), mod_ffn, i, 0, mod_row, norm_g, gi, w1, w3, w2, i, lj,
                      rows, final_g=final)
        return jnp.pad(y.reshape(b, valid, d), ((0, 0), (0, seq - valid), (0, 0)))

    new_state, new_ssd_conv, new_cmod_conv = [], [], []
    for i in range(depth):
        x = ffn(x, i, 0, 3 * i, 0)
        j = i // 2
        if i % 2 == 0:
            x, cache, st = _ssd_call(x, mod, i, boff, norm_g, 3 * i + 1, win, wdt, cw, cb, dtb, alog, dexp, ssd_ng,
                                     wout, ehp, ehs, ssd_conv_cache[j], ssd_state_t[j], j, tm_ssd,
                                     min(valid, tm_ssd), heads, n)
            new_ssd_conv.append(cache)
            new_state.append(st)
        else:
            x, cache = _cmod_call(x, mod, i, boff, norm_g, 3 * i + 1, wp1, bp1, dww, dwb, lng, lnb, wp2, bp2,
                                  cmod_conv_cache[j], j, tm_conv, min(valid, tm_conv))
            new_cmod_conv.append(cache)
        x = ffn(x, i, 6, 3 * i + 2, 1, final_g if i == depth - 1 else None)
    return x, jnp.stack(new_state), jnp.stack(new_ssd_conv), jnp.stack(new_cmod_conv)


def kernel(x_prompt, x_sample, c_prompt, c_sample, state_ssd, cache_ssd_conv, cache_cmod_conv, mod_w, mod_b, norm_g, ffn_w1, ffn_w3, ffn_w2, ssd_w_in, ssd_conv_w, ssd_conv_b, ssd_dt_bias, ssd_a_log, ssd_d, ssd_norm_g, ssd_w_out, cmod_w_pw1, cmod_b_pw1, cmod_dw_w, cmod_dw_b, cmod_ln_g, cmod_ln_b, cmod_w_pw2, cmod_b_pw2, final_g):
    bp, sp, d = x_prompt.shape
    bs, ss, _ = x_sample.shape
    n_ssd, _, heads, p, n = state_ssd.shape
    di = heads * p
    cd = ssd_conv_w.shape[-1]
    depth = mod_w.shape[0]
    assert heads <= LANES and ss <= SSD_CHUNK

    mod = _mod_call(jnp.concatenate([c_prompt, c_sample], axis=0), mod_w, mod_b)
    mod = mod.reshape(depth, bp + bs, N_MOD, d)

    lane_pad = lambda a: jnp.pad(a, ((0, 0), (0, LANES - a.shape[-1])))
    win = ssd_w_in.astype(BF16)
    wdt = jnp.pad(ssd_w_in[..., di + cd:], ((0, 0), (0, 0), (0, LANES - heads))).astype(BF16)
    ehp, ehs = _head_expanders(heads, p, SSD_CHUNK)
    wts = (norm_g.reshape(depth * 3, 1, d), ffn_w1.astype(BF16), ffn_w3.astype(BF16), ffn_w2.astype(BF16),
           win, wdt, ssd_conv_w, ssd_conv_b, lane_pad(ssd_dt_bias), lane_pad(ssd_a_log),
           jnp.repeat(ssd_d, p, axis=-1), ssd_norm_g, ssd_w_out.astype(BF16), ehp, ehs,
           cmod_w_pw1.astype(BF16), cmod_b_pw1, cmod_dw_w, cmod_dw_b, cmod_ln_g, cmod_ln_b,
           cmod_w_pw2.astype(BF16), cmod_b_pw2, final_g.reshape(1, d), heads, n)

    def to_t(st):
        return jnp.swapaxes(st.reshape(st.shape[0], st.shape[1], di, n), 2, 3)

    def from_t(st_t):
        return jnp.swapaxes(st_t, 2, 3).reshape(st_t.shape[0], st_t.shape[1], heads, p, n)

    cmod_pad = 32 - cache_cmod_conv.shape[2]
    ssd_pad = SUBLANES - cache_ssd_conv.shape[2]

    zeros = lambda *shape: jnp.zeros(shape, F32)
    y_p, st_p, sc_p, cc_p = _trunk(
        x_prompt, mod, 0, sp,
        zeros(n_ssd, bp, n, di), zeros(n_ssd, bp, SUBLANES, cd),
        zeros(cache_cmod_conv.shape[0], bp, 32, cache_cmod_conv.shape[-1]), wts, flat_ffn=False)

    xs_pad = jnp.pad(x_sample, ((0, 0), (0, SSD_CHUNK - ss), (0, 0)))
    y_s, st_s, sc_s, cc_s = _trunk(
        xs_pad, mod, bp, ss,
        to_t(state_ssd), jnp.pad(cache_ssd_conv, ((0, 0), (0, 0), (ssd_pad, 0), (0, 0))),
        jnp.pad(cache_cmod_conv, ((0, 0), (0, 0), (cmod_pad, 0), (0, 0))), wts,
        flat_ffn=(bs * ss) % SUBLANES == 0)

    return (y_p, y_s[:, :ss], from_t(st_p), sc_p, cc_p, from_t(st_s), sc_s, cc_s)
```
